```python
import jax, jax.numpy as jnp
from jax import lax
import numpy as np

D_MODEL = 1024
BATCH = 32
SEQ = 256
DEPTH = 1
DEC_BATCH = 4
DEC_SEQ = 2048
PAST_LEN = 512

GRID_W = 64
CHUNK = 128
EPS = 1e-6
N_MOD = 6
N_BRANCH = 2
SSD_HEADS = 16
SSD_HEAD_DIM = 64
SSD_WIDTH = SSD_HEADS * SSD_HEAD_DIM
SSD_STATE = 128
SSD_GROUPS = 4
CONV_K = 3
CONV_CH = SSD_WIDTH + 2 * SSD_GROUPS * SSD_STATE
MLSTM_HEADS = 4
MLSTM_HEAD_DIM = 256
MLSTM_WIDTH = MLSTM_HEADS * MLSTM_HEAD_DIM
PEER_HEADS = 8
PEER_KEY_DIM = 256
PEER_N_KEYS = 128
PEER_TOPK = 16
PEER_EXPERTS = PEER_N_KEYS * PEER_N_KEYS
PEER_BLOCK = 128
IN_SPLITS = (SSD_WIDTH, CONV_CH, 2 * SSD_HEADS, MLSTM_WIDTH, MLSTM_WIDTH, MLSTM_WIDTH, MLSTM_WIDTH, 4 * MLSTM_HEADS, N_BRANCH * D_MODEL)
IN_COLS = SSD_WIDTH + CONV_CH + 2 * SSD_HEADS + 4 * MLSTM_WIDTH + 4 * MLSTM_HEADS + N_BRANCH * D_MODEL

kernel_name = 'bidir_ssd_mlstm_peer_prefix_step'


def rms_norm(x, g):
    xf = x.astype(jnp.float32)
    y = xf * lax.rsqrt(jnp.mean(xf * xf, axis=-1, keepdims=True) + EPS)
    return (y * g.astype(jnp.float32)).astype(x.dtype)


def flip(t):
    return jnp.flip(t, axis=1)


def adaln(cvec, w, b):
    return jax.nn.silu(cvec) @ w + b


def dwconv_centred(u, w, b):
    pad = CONV_K // 2
    n = u.shape[-2]
    up = jnp.pad(u, [(0, 0)] * (u.ndim - 2) + [(pad, pad), (0, 0)])
    out = b + w[0] * up[..., 0:n, :]
    for j in range(1, CONV_K):
        out = out + w[j] * up[..., j:j + n, :]
    return out


def ssd_scan(x, dt, a, b_in, c_in, h0):
    bsz, n, nh, hp = x.shape
    nc = n // CHUNK
    rep = nh // b_in.shape[2]
    x = x.astype(jnp.float32)
    dt = dt.astype(jnp.float32)
    bh = jnp.repeat(b_in.astype(jnp.float32), rep, axis=2).reshape(bsz, nc, CHUNK, nh, -1)
    ch = jnp.repeat(c_in.astype(jnp.float32), rep, axis=2).reshape(bsz, nc, CHUNK, nh, -1)
    xdt = (x * dt[..., None]).reshape(bsz, nc, CHUNK, nh, hp)
    acs = jnp.cumsum((dt * a).reshape(bsz, nc, CHUNK, nh).transpose(0, 3, 1, 2), axis=-1)
    causal = jnp.tril(jnp.ones((CHUNK, CHUNK), dtype=bool))
    lmat = jnp.exp(jnp.where(causal, acs[..., :, None] - acs[..., None, :], -jnp.inf))
    scores = jnp.einsum('bclhn,bcshn->bhcls', ch, bh) * lmat
    y_diag = jnp.einsum('bhcls,bcshp->bclhp', scores, xdt)
    decay_end = jnp.exp(acs[..., -1:] - acs)
    chunk_states = jnp.einsum('bclhn,bhcl,bclhp->bchpn', bh, decay_end, xdt)
    chunk_decay = jnp.exp(acs[..., -1])

    def step(h, inp):
        s_c, d_c = inp
        return d_c[..., None, None] * h + s_c, h

    h_fin, h_prev = lax.scan(step, h0.astype(jnp.float32),
                             (chunk_states.transpose(1, 0, 2, 3, 4), chunk_decay.transpose(2, 0, 1)))
    h_prev = h_prev.transpose(1, 0, 2, 3, 4)
    y_off = jnp.einsum('bclhn,bchpn,bhcl->bclhp', ch, h_prev, jnp.exp(acs))
    return (y_diag + y_off).reshape(bsz, n, nh, hp), h_fin


def mlstm_scan(q, k, v, log_i, log_f, c0, n0, m0):
    bsz, n, nh, dh = q.shape
    nc = n // CHUNK

    def chunks(t):
        return t.astype(jnp.float32).reshape(bsz, nc, CHUNK, nh, dh).transpose(0, 3, 1, 2, 4)

    qc, kc, vc = chunks(q), chunks(k) * (dh ** -0.5), chunks(v)
    ic = log_i.astype(jnp.float32).reshape(bsz, nc, CHUNK, nh).transpose(0, 3, 1, 2)
    fc = log_f.astype(jnp.float32).reshape(bsz, nc, CHUNK, nh).transpose(0, 3, 1, 2)
    bcum = jnp.cumsum(fc, axis=-1)
    causal = jnp.tril(jnp.ones((CHUNK, CHUNK), dtype=bool))
    dlog = jnp.where(causal, bcum[..., :, None] - bcum[..., None, :] + ic[..., None, :], -jnp.inf)
    g = bcum[..., -1:] - bcum + ic
    m_loc = jnp.max(g, axis=-1)
    w_loc = jnp.exp(g - m_loc[..., None])
    c_loc = jnp.einsum('bhcs,bhcsk,bhcsv->bhckv', w_loc, kc, vc)
    n_loc = jnp.einsum('bhcs,bhcsk->bhck', w_loc, kc)
    f_tot = bcum[..., -1]

    def step(carry, inp):
        cm, nm, mm = carry
        cl, nl, ml, ft = inp
        m_new = jnp.maximum(ft + mm, ml)
        wa = jnp.exp(ft + mm - m_new)
        wb = jnp.exp(ml - m_new)
        c_new = wa[..., None, None] * cm + wb[..., None, None] * cl
        n_new = wa[..., None] * nm + wb[..., None] * nl
        return (c_new, n_new, m_new), (cm, nm, mm)

    init = (c0.astype(jnp.float32), n0.astype(jnp.float32), m0.astype(jnp.float32))
    (c_f, n_f, m_f), (c_p, n_p, m_p) = lax.scan(
        step, init, (c_loc.transpose(2, 0, 1, 3, 4), n_loc.transpose(2, 0, 1, 3),
                     m_loc.transpose(2, 0, 1), f_tot.transpose(2, 0, 1)))
    c_p = c_p.transpose(1, 2, 0, 3, 4)
    n_p = n_p.transpose(1, 2, 0, 3)
    m_p = m_p.transpose(1, 2, 0)
    inter = bcum + m_p[..., None]
    m_t = jnp.maximum(inter, jnp.max(dlog, axis=-1))
    s = jnp.einsum('bhcld,bhcsd->bhcls', qc, kc) * jnp.exp(dlog - m_t[..., None])
    w_int = jnp.exp(inter - m_t)
    num = jnp.einsum('bhcls,bhcsd->bhcld', s, vc) + w_int[..., None] * jnp.einsum('bhcld,bhcdv->bhclv', qc, c_p)
    den = jnp.sum(s, axis=-1) + w_int * jnp.einsum('bhcld,bhcd->bhcl', qc, n_p)
    h = num / jnp.maximum(jnp.abs(den), jnp.exp(-m_t))[..., None]
    h = h.transpose(0, 2, 3, 1, 4).reshape(bsz, n, nh, dh)
    return h, (c_f, n_f, m_f)


def peer_ffn(h, w_q, keys1, keys2, u_tab, v_tab):
    bsz, n, d = h.shape
    tok = h.reshape(-1, PEER_BLOCK, d)

    def block(xb):
        q = (xb @ w_q).reshape(PEER_BLOCK, PEER_HEADS, 2, PEER_KEY_DIM // 2).astype(jnp.float32)
        s1 = jnp.einsum('thd,nd->thn', q[:, :, 0], keys1.astype(jnp.float32))
        s2 = jnp.einsum('thd,nd->thn', q[:, :, 1], keys2.astype(jnp.float32))
        v1, i1 = lax.top_k(s1, PEER_TOPK)
        v2, i2 = lax.top_k(s2, PEER_TOPK)
        cand = (v1[..., :, None] + v2[..., None, :]).reshape(PEER_BLOCK, PEER_HEADS, PEER_TOPK * PEER_TOPK)
        sc, ci = lax.top_k(cand, PEER_TOPK)
        e = (jnp.take_along_axis(i1, ci // PEER_TOPK, axis=-1) * PEER_N_KEYS
             + jnp.take_along_axis(i2, ci % PEER_TOPK, axis=-1))
        gate = jax.nn.softmax(sc, axis=-1)
        act = jax.nn.gelu(jnp.einsum('thkd,td->thk', u_tab[e], xb), approximate=False)
        return jnp.einsum('thk,thkd->td', gate * act, v_tab[e])

    return lax.map(block, tok).reshape(bsz, n, d)


def trunk_layer(x, mod, on_grid, ssd_h0, mlstm_c0, mlstm_n0, mlstm_m0, lp):
    bsz, n, _ = x.shape
    sh_a, sc_a, gt_a, sh_f, sc_f, gt_f = jnp.split(mod[:, None, :], N_MOD, axis=-1)
    h = rms_norm(x, lp['g_mix_pre']) * (1.0 + sc_a) + sh_a
    proj = h @ lp['w_in']
    z, xbc, dt_raw, q, k, v, o, gate_raw, merge_raw = jnp.split(
        proj, np.cumsum(IN_SPLITS)[:-1].tolist(), axis=-1)
    if on_grid:
        rows = n // GRID_W
        xbc = dwconv_centred(xbc.reshape(bsz, rows, GRID_W, CONV_CH), lp['conv_w'], lp['conv_b']).reshape(bsz, n, CONV_CH)
    else:
        xbc = dwconv_centred(xbc, lp['conv_w'], lp['conv_b'])
    xbc = jax.nn.silu(xbc)
    xs, bs, cs = jnp.split(xbc, [SSD_WIDTH, SSD_WIDTH + SSD_GROUPS * SSD_STATE], axis=-1)
    xs = xs.reshape(bsz, n, SSD_HEADS, SSD_HEAD_DIM)
    bs = bs.reshape(bsz, n, SSD_GROUPS, SSD_STATE)
    cs = cs.reshape(bsz, n, SSD_GROUPS, SSD_STATE)
    dt = jax.nn.softplus(dt_raw.reshape(bsz, n, 2, SSD_HEADS).astype(jnp.float32) + lp['ssd_dt_bias'].astype(jnp.float32))
    a = -jnp.exp(lp['ssd_a_log'].astype(jnp.float32))
    y_f, ssd_f = ssd_scan(xs, dt[:, :, 0], a[0], bs, cs, ssd_h0[:, 0])
    y_b, ssd_b = ssd_scan(flip(xs), flip(dt[:, :, 1]), a[1], flip(bs), flip(cs), ssd_h0[:, 1])
    y = y_f + flip(y_b) + (lp['ssd_d'][0] + lp['ssd_d'][1])[:, None] * xs
    y = rms_norm(y.reshape(bsz, n, SSD_WIDTH) * jax.nn.silu(z), lp['ssd_norm_g'])
    y_ssd = y @ lp['w_ssd_out']
    q = q.reshape(bsz, n, MLSTM_HEADS, MLSTM_HEAD_DIM)
    k = k.reshape(bsz, n, MLSTM_HEADS, MLSTM_HEAD_DIM)
    v = v.reshape(bsz, n, MLSTM_HEADS, MLSTM_HEAD_DIM)
    g = gate_raw.reshape(bsz, n, 2, 2, MLSTM_HEADS).astype(jnp.float32)
    log_i = g[:, :, :, 0] + lp['mlstm_i_bias']
    log_f = jax.nn.log_sigmoid(g[:, :, :, 1] + lp['mlstm_f_bias'])
    hm_f, (cf, nf, mf) = mlstm_scan(q, k, v, log_i[:, :, 0], log_f[:, :, 0],
                                    mlstm_c0[:, 0], mlstm_n0[:, 0], mlstm_m0[:, 0])
    hm_b, (cb, nb, mb) = mlstm_scan(flip(q), flip(k), flip(v), flip(log_i[:, :, 1]), flip(log_f[:, :, 1]),
                                    mlstm_c0[:, 1], mlstm_n0[:, 1], mlstm_m0[:, 1])
    hm = rms_norm(hm_f + flip(hm_b), lp['mlstm_norm_g'].reshape(MLSTM_HEADS, MLSTM_HEAD_DIM))
    hm = hm.reshape(bsz, n, MLSTM_WIDTH) * jax.nn.sigmoid(o)
    y_ml = hm @ lp['w_mlstm_out']
    gate_ssd, gate_ml = jnp.split(jax.nn.sigmoid(merge_raw), N_BRANCH, axis=-1)
    mixed = (gate_ssd * y_ssd + gate_ml * y_ml) @ lp['w_out']
    x1 = x + gt_a * rms_norm(mixed, lp['g_mix_post'])
    h2 = rms_norm(x1, lp['g_ffn_pre']) * (1.0 + sc_f) + sh_f
    f = peer_ffn(h2, lp['peer_w_q'], lp['peer_keys1'], lp['peer_keys2'], lp['peer_u'], lp['peer_v'])
    x2 = x1 + gt_f * rms_norm(f, lp['g_ffn_post'])
    return (x2.astype(x.dtype), jnp.stack([ssd_f, ssd_b], axis=1), jnp.stack([cf, cb], axis=1),
            jnp.stack([nf, nb], axis=1), jnp.stack([mf, mb], axis=1))


def setup_inputs(seed: int = 0) -> dict:
    key = jax.random.key(seed)
    ks = jax.random.split(key, 40)

    def nrm(k, shape, scale):
        return jax.random.normal(k, shape, jnp.float32) * scale

    def gain(k, shape):
        return 1.0 + nrm(k, shape, 0.02)

    dt0 = jnp.exp(jax.random.uniform(ks[18], (DEPTH, 2, SSD_HEADS), jnp.float32, np.log(1e-3), np.log(1e-1)))
    return {
        'x_prompt': nrm(ks[0], (BATCH, SEQ, D_MODEL), 1.0),
        'x_sample': nrm(ks[1], (DEC_BATCH, DEC_SEQ, D_MODEL), 1.0),
        'c': nrm(ks[2], (DEC_BATCH, D_MODEL), 1.0),
        'state_ssd': nrm(ks[3], (DEC_BATCH, DEPTH, 2, SSD_HEADS, SSD_HEAD_DIM, SSD_STATE), 0.1),
        'state_mlstm_c': nrm(ks[4], (DEC_BATCH, DEPTH, 2, MLSTM_HEADS, MLSTM_HEAD_DIM, MLSTM_HEAD_DIM), 0.1),
        'state_mlstm_n': nrm(ks[5], (DEC_BATCH, DEPTH, 2, MLSTM_HEADS, MLSTM_HEAD_DIM), 0.1),
        'state_mlstm_m': nrm(ks[6], (DEC_BATCH, DEPTH, 2, MLSTM_HEADS), 1.0),
        'c_ctx': nrm(ks[7], (D_MODEL,), 1.0),
        'w_mod': nrm(ks[8], (DEPTH, D_MODEL, N_MOD * D_MODEL), 0.5 * D_MODEL ** -0.5),
        'b_mod': nrm(ks[9], (DEPTH, N_MOD * D_MODEL), 0.02),
        'g_mix_pre': gain(ks[10], (DEPTH, D_MODEL)),
        'g_mix_post': gain(ks[11], (DEPTH, D_MODEL)),
        'g_ffn_pre': gain(ks[12], (DEPTH, D_MODEL)),
        'g_ffn_post': gain(ks[13], (DEPTH, D_MODEL)),
        'w_in': nrm(ks[14], (DEPTH, D_MODEL, IN_COLS), D_MODEL ** -0.5),
        'conv_w': nrm(ks[15], (DEPTH, CONV_K, CONV_CH), CONV_K ** -0.5),
        'conv_b': nrm(ks[16], (DEPTH, CONV_CH), 0.02),
        'ssd_a_log': jnp.log(jax.random.uniform(ks[17], (DEPTH, 2, SSD_HEADS), jnp.float32, 1.0, 16.0)),
        'ssd_dt_bias': dt0 + jnp.log(-jnp.expm1(-dt0)),
        'ssd_d': 1.0 + nrm(ks[19], (DEPTH, 2, SSD_HEADS), 0.1),
        'ssd_norm_g': gain(ks[20], (DEPTH, SSD_WIDTH)),
        'w_ssd_out': nrm(ks[21], (DEPTH, SSD_WIDTH, D_MODEL), SSD_WIDTH ** -0.5),
        'mlstm_i_bias': nrm(ks[22], (DEPTH, 2, MLSTM_HEADS), 0.1),
        'mlstm_f_bias': jnp.linspace(3.0, 6.0, MLSTM_HEADS, dtype=jnp.float32) + nrm(ks[23], (DEPTH, 2, MLSTM_HEADS), 0.1),
        'mlstm_norm_g': gain(ks[24], (DEPTH, MLSTM_WIDTH)),
        'w_mlstm_out': nrm(ks[25], (DEPTH, MLSTM_WIDTH, D_MODEL), MLSTM_WIDTH ** -0.5),
        'w_out': nrm(ks[26], (DEPTH, D_MODEL, D_MODEL), D_MODEL ** -0.5),
        'peer_w_q': nrm(ks[27], (DEPTH, D_MODEL, PEER_HEADS * PEER_KEY_DIM), D_MODEL ** -0.5),
        'peer_keys1': nrm(ks[28], (DEPTH, PEER_N_KEYS, PEER_KEY_DIM // 2), (PEER_KEY_DIM // 2) ** -0.5),
        'peer_keys2': nrm(ks[29], (DEPTH, PEER_N_KEYS, PEER_KEY_DIM // 2), (PEER_KEY_DIM // 2) ** -0.5),
        'peer_u': nrm(ks[30], (DEPTH, PEER_EXPERTS, D_MODEL), D_MODEL ** -0.5),
        'peer_v': nrm(ks[31], (DEPTH, PEER_EXPERTS, D_MODEL), PEER_HEADS ** -0.5),
    }


def reference(x_prompt, x_sample, c, state_ssd, state_mlstm_c, state_mlstm_n, state_mlstm_m, c_ctx,
              w_mod, b_mod, g_mix_pre, g_mix_post, g_ffn_pre, g_ffn_post, w_in, conv_w, conv_b,
              ssd_a_log, ssd_dt_bias, ssd_d, ssd_norm_g, w_ssd_out, mlstm_i_bias, mlstm_f_bias,
              mlstm_norm_g, w_mlstm_out, w_out, peer_w_q, peer_keys1, peer_keys2, peer_u, peer_v):
    bp = x_prompt.shape[0]
    yp = x_prompt
    ys = x_sample
    out_ssd, out_c, out_n, out_m = [], [], [], []
    for l in range(DEPTH):
        lp = {
            'g_mix_pre': g_mix_pre[l], 'g_mix_post': g_mix_post[l],
            'g_ffn_pre': g_ffn_pre[l], 'g_ffn_post': g_ffn_post[l],
            'w_in': w_in[l], 'conv_w': conv_w[l], 'conv_b': conv_b[l],
            'ssd_a_log': ssd_a_log[l], 'ssd_dt_bias': ssd_dt_bias[l], 'ssd_d': ssd_d[l],
            'ssd_norm_g': ssd_norm_g[l], 'w_ssd_out': w_ssd_out[l],
            'mlstm_i_bias': mlstm_i_bias[l], 'mlstm_f_bias': mlstm_f_bias[l],
            'mlstm_norm_g': mlstm_norm_g[l], 'w_mlstm_out': w_mlstm_out[l], 'w_out': w_out[l],
            'peer_w_q': peer_w_q[l], 'peer_keys1': peer_keys1[l], 'peer_keys2': peer_keys2[l],
            'peer_u': peer_u[l], 'peer_v': peer_v[l],
        }
        mod_ctx = jnp.broadcast_to(adaln(c_ctx, w_mod[l], b_mod[l])[None, :], (bp, N_MOD * D_MODEL))
        ssd0 = jnp.zeros((bp, 2, SSD_HEADS, SSD_HEAD_DIM, SSD_STATE), jnp.float32)
        c0 = jnp.zeros((bp, 2, MLSTM_HEADS, MLSTM_HEAD_DIM, MLSTM_HEAD_DIM), jnp.float32)
        n0 = jnp.zeros((bp, 2, MLSTM_HEADS, MLSTM_HEAD_DIM), jnp.float32)
        m0 = jnp.zeros((bp, 2, MLSTM_HEADS), jnp.float32)
        yp, s_ssd, s_c, s_n, s_m = trunk_layer(yp, mod_ctx, False, ssd0, c0, n0, m0, lp)
        out_ssd.append(s_ssd)
        out_c.append(s_c)
        out_n.append(s_n)
        out_m.append(s_m)
        mod_lat = adaln(c, w_mod[l], b_mod[l])
        ys, _, _, _, _ = trunk_layer(ys, mod_lat, True, state_ssd[:, l], state_mlstm_c[:, l],
                                     state_mlstm_n[:, l], state_mlstm_m[:, l], lp)
    new_state_ssd = jnp.stack(out_ssd, axis=1)
    new_state_mlstm_c = jnp.stack(out_c, axis=1)
    new_state_mlstm_n = jnp.stack(out_n, axis=1)
    new_state_mlstm_m = jnp.stack(out_m, axis=1)
    return (yp, ys, new_state_ssd, new_state_mlstm_c, new_state_mlstm_n, new_state_mlstm_m)
```

```python
import functools

import numpy as np
import jax
import jax.numpy as jnp
from jax import lax
from jax.experimental import pallas as pl
from jax.experimental.pallas import tpu as pltpu

F32 = jnp.float32
BF16 = jnp.bfloat16
HIGHEST = lax.Precision.HIGHEST

D_MODEL = 1024
EPS = 1e-6
N_MOD = 6
CHUNK = 128
SEQ = 256
GRID_W = 64
SSD_HEADS = 16
SSD_HEAD_DIM = 64
SSD_STATE = 128
SSD_GROUPS = 4
SSD_WIDTH = SSD_HEADS * SSD_HEAD_DIM
MLSTM_HEADS = 4
MLSTM_HEAD_DIM = 256
PEER_HEADS = 8
PEER_N_KEYS = 128
PEER_TOPK = 16
PEER_EXPERTS = PEER_N_KEYS * PEER_N_KEYS
SMALL_COLS = 128
BIG_COLS = 9 * D_MODEL
NEG = -1e30

VMEM_LIMIT = 56 * 1024 * 1024

_CAND = [(j, l) for j in range(PEER_TOPK) for l in range(PEER_TOPK) if (j + 1) * (l + 1) <= PEER_TOPK]
N_CAND = len(_CAND)
N_CAND_PAD = 56


def _cparams(sem):
    return pltpu.CompilerParams(dimension_semantics=sem, vmem_limit_bytes=VMEM_LIMIT)


def _sigmoid(x):
    return 1.0 / (1.0 + jnp.exp(-x))


def _silu(x):
    return x * _sigmoid(x)


def _rms(x, g):
    return x * lax.rsqrt(jnp.mean(x * x, axis=-1, keepdims=True) + EPS) * g


def _adaln_body(c_ref, w_ref, b_ref, o_ref):
    cv = c_ref[...]
    o_ref[...] = jnp.dot(_silu(cv), w_ref[...], precision=HIGHEST, preferred_element_type=F32) + b_ref[...]


def _adaln(cvec, w_mod, b_mod):
    tn = 512
    n = w_mod.shape[1]
    return pl.pallas_call(
        _adaln_body,
        grid=(n // tn,),
        in_specs=[pl.BlockSpec((8, D_MODEL), lambda j: (0, 0)),
                  pl.BlockSpec((D_MODEL, tn), lambda j: (0, j)),
                  pl.BlockSpec((1, tn), lambda j: (0, j))],
        out_specs=pl.BlockSpec((8, tn), lambda j: (0, j)),
        out_shape=jax.ShapeDtypeStruct((8, n), F32),
        compiler_params=_cparams(("arbitrary",)),
        name="adaln",
    )(cvec, w_mod, b_mod.reshape(1, n))


IP_TM = 1024
IP_TN = 512


def _inproj_body(n_prompt_tiles, x_ref, sh_ref, sc_ref, g_ref, w_ref, ws_ref, bs_ref, cw_ref,
                 p_ref, s_ref, h_scr):
    i = pl.program_id(0)
    j = pl.program_id(1)

    @pl.when(j == 0)
    def _():
        h = _rms(x_ref[...], g_ref[...]) * (1.0 + sc_ref[...]) + sh_ref[...]
        h_scr[...] = h.astype(BF16)
        sm = jnp.dot(h, ws_ref[...], precision=HIGHEST, preferred_element_type=F32) + bs_ref[...]
        col = lax.broadcasted_iota(jnp.int32, sm.shape, 1)
        is_f = (col >= 32) & (col < 48) & ((((col - 32) // MLSTM_HEADS) % 2) == 1)
        lp = jnp.log1p(jnp.exp(-jnp.abs(sm)))
        softplus = jnp.maximum(sm, 0.0) + lp
        logsig = jnp.minimum(sm, 0.0) - lp
        s_ref[...] = jnp.where(col < 32, softplus, jnp.where(is_f, logsig, sm))

    acc = jnp.dot(h_scr[...], w_ref[...], preferred_element_type=F32)

    @pl.when(j < 2)
    def _():
        p_ref[...] = _silu(acc).astype(BF16)

    @pl.when((j >= 2) & (j < 6))
    def _():
        seg_mask = jnp.where(i < n_prompt_tiles, SEQ - 1, GRID_W - 1)
        pos = lax.broadcasted_iota(jnp.int32, acc.shape, 0) & seg_mask
        prev = jnp.where(pos == 0, 0.0, pltpu.roll(acc, 1, 0))
        nxt = jnp.where(pos == seg_mask, 0.0, pltpu.roll(acc, IP_TM - 1, 0))
        cw = cw_ref[...]
        u = cw[3:4] + cw[0:1] * prev + cw[1:2] * acc + cw[2:3] * nxt
        p_ref[...] = _silu(u).astype(BF16)

    @pl.when((j >= 6) & (j < 12))
    def _():
        p_ref[...] = acc.astype(BF16)

    @pl.when(j >= 12)
    def _():
        p_ref[...] = _sigmoid(acc).astype(BF16)


def _inproj(x, mod4, g, w_big, w_small, b_small, cw, row_of_tile, n_prompt_tiles):
    t = x.shape[0]
    grid = (t // IP_TM, BIG_COLS // IP_TN)
    return pl.pallas_call(
        functools.partial(_inproj_body, n_prompt_tiles),
        grid=grid,
        in_specs=[pl.BlockSpec((IP_TM, D_MODEL), lambda i, j: (i, 0)),
                  pl.BlockSpec((None, None, 1, D_MODEL), lambda i, j: (row_of_tile(i), 0, 0, 0)),
                  pl.BlockSpec((None, None, 1, D_MODEL), lambda i, j: (row_of_tile(i), 1, 0, 0)),
                  pl.BlockSpec((1, D_MODEL), lambda i, j: (0, 0)),
                  pl.BlockSpec((D_MODEL, IP_TN), lambda i, j: (0, j)),
                  pl.BlockSpec((D_MODEL, SMALL_COLS), lambda i, j: (0, 0)),
                  pl.BlockSpec((1, SMALL_COLS), lambda i, j: (0, 0)),
                  pl.BlockSpec((8, IP_TN), lambda i, j: (0, j))],
        out_specs=[pl.BlockSpec((IP_TM, IP_TN), lambda i, j: (i, j)),
                   pl.BlockSpec((IP_TM, SMALL_COLS), lambda i, j: (i, 0))],
        out_shape=[jax.ShapeDtypeStruct((t, BIG_COLS), BF16),
                   jax.ShapeDtypeStruct((t, SMALL_COLS), F32)],
        scratch_shapes=[pltpu.VMEM((IP_TM, D_MODEL), BF16)],
        compiler_params=_cparams(("parallel", "arbitrary")),
        name="inproj",
    )(x, mod4, mod4, g, w_big, w_small, b_small, cw)


def _ssd_body(has_h0, emit_state, *refs):
    refs = list(refs)
    xs_ref, bc_ref, sm_ref, tri_ref, alog_ref = refs[:5]
    k = 5
    h0_ref = None
    if has_h0:
        h0_ref = refs[k]
        k += 1
    y_ref = refs[k]
    k += 1
    hfin_ref = None
    if emit_state:
        hfin_ref = refs[k]
        k += 1
    hs_scr = refs[k]

    d = pl.program_id(0)
    c = pl.program_id(2)
    nc = pl.num_programs(2)

    @pl.when(c == 0)
    def _():
        if has_h0:
            hs_scr[...] = h0_ref[...].T
        else:
            hs_scr[...] = jnp.zeros(hs_scr.shape, F32)

    tri = tri_ref[...]
    keep = tri > 0.0
    sm = sm_ref[...]
    lane = lax.broadcasted_iota(jnp.int32, sm.shape, 1)
    dt = jnp.where(d == 0, sm, pltpu.roll(sm, SMALL_COLS - SSD_HEADS, 1))
    a_row = -jnp.exp(alog_ref[...])
    dta = jnp.where(lane < SSD_HEADS, dt * a_row, 0.0)
    cum = jnp.dot(tri, dta, precision=HIGHEST, preferred_element_type=F32)
    cum_t = cum.T
    tot = jnp.where(d == 0, cum[CHUNK - 1:CHUNK, :], cum[0:1, :])
    decend = jnp.exp(tot - cum)
    dec_tot = jnp.exp(tot)

    bc = bc_ref[...]
    for g in range(SSD_GROUPS):
        b_g = bc[:, g * SSD_STATE:(g + 1) * SSD_STATE]
        c_g = bc[:, (SSD_GROUPS + g) * SSD_STATE:(SSD_GROUPS + g + 1) * SSD_STATE]
        cb = lax.dot_general(c_g, b_g, (((1,), (1,)), ((), ())), preferred_element_type=F32)
        c_gf = c_g.astype(F32)
        b_gt = b_g.astype(F32).T.astype(BF16)
        for hh in range(SSD_HEADS // SSD_GROUPS):
            h = g * (SSD_HEADS // SSD_GROUPS) + hh
            sl = slice(h * SSD_HEAD_DIM, (h + 1) * SSD_HEAD_DIM)
            cum_c = cum[:, h:h + 1]
            lmat = jnp.exp(jnp.where(keep, cum_c - cum_t[h:h + 1, :], NEG))
            s_h = (cb * lmat).astype(BF16)
            c_e = (c_gf * jnp.exp(cum_c)).astype(BF16)
            xdt = xs_ref[:, sl].astype(F32) * dt[:, h:h + 1]
            hs_h = hs_scr[:, sl]
            lhs = jnp.concatenate([s_h, c_e], axis=1)
            rhs = jnp.concatenate([xdt.astype(BF16), hs_h.astype(BF16)], axis=0)
            y_ref[:, sl] = jnp.dot(lhs, rhs, preferred_element_type=F32)
            xdec = (xdt * decend[:, h:h + 1]).astype(BF16)
            upd = jnp.dot(b_gt, xdec, preferred_element_type=F32)
            hs_scr[:, sl] = dec_tot[:, h:h + 1] * hs_h + upd

    if emit_state:
        @pl.when(c == nc - 1)
        def _():
            hfin_ref[...] = hs_scr[...].T


def _chunk_index(cbase, nc):
    def f(d, b, c):
        return cbase + b * nc + c + d * (nc - 1 - 2 * c)
    return f


def _ssd(p_big, small, tri, a_log, h0, cbase, nseq, nc, emit_state):
    ci = _chunk_index(cbase, nc)
    co = _chunk_index(0, nc)
    has_h0 = h0 is not None
    in_specs = [pl.BlockSpec((CHUNK, D_MODEL), lambda d, b, c: (ci(d, b, c), 1)),
                pl.BlockSpec((CHUNK, D_MODEL), lambda d, b, c: (ci(d, b, c), 2)),
                pl.BlockSpec((CHUNK, SMALL_COLS), lambda d, b, c: (ci(d, b, c), 0)),
                pl.BlockSpec((None, CHUNK, CHUNK), lambda d, b, c: (d, 0, 0)),
                pl.BlockSpec((None, 1, SMALL_COLS), lambda d, b, c: (d, 0, 0))]
    args = [p_big, p_big, small, tri, a_log]
    if has_h0:
        in_specs.append(pl.BlockSpec((None, None, SSD_WIDTH, SSD_STATE), lambda d, b, c: (b, d, 0, 0)))
        args.append(h0)
    out_specs = [pl.BlockSpec((None, CHUNK, SSD_WIDTH), lambda d, b, c: (d, co(d, b, c), 0))]
    out_shape = [jax.ShapeDtypeStruct((2, nseq * nc * CHUNK, SSD_WIDTH), F32)]
    if emit_state:
        out_specs.append(pl.BlockSpec((None, None, SSD_WIDTH, SSD_STATE), lambda d, b, c: (b, d, 0, 0)))
        out_shape.append(jax.ShapeDtypeStruct((nseq, 2, SSD_WIDTH, SSD_STATE), F32))
    return pl.pallas_call(
        functools.partial(_ssd_body, has_h0, emit_state),
        grid=(2, nseq, nc),
        in_specs=in_specs,
        out_specs=out_specs,
        out_shape=out_shape,
        scratch_shapes=[pltpu.VMEM((SSD_STATE, SSD_WIDTH), F32)],
        compiler_params=_cparams(("arbitrary", "arbitrary", "arbitrary")),
        name="ssd_state" if emit_state else "ssd_seeded",
    )(*args)


def _mlstm_body(has_init, emit_state, *refs):
    refs = list(refs)
    q_ref, k_ref, v_ref, gc_ref, gr_ref, tri_ref, trit_ref = refs[:7]
    k_i = 7
    c0_ref = n0_ref = m0_ref = None
    if has_init:
        c0_ref, n0_ref, m0_ref = refs[k_i:k_i + 3]
        k_i += 3
    hm_ref = refs[k_i]
    k_i += 1
    cfin_ref = nfin_ref = mfin_ref = None
    if emit_state:
        cfin_ref, nfin_ref, mfin_ref = refs[k_i:k_i + 3]
        k_i += 3
    c_scr, n_scr, m_scr = refs[k_i:k_i + 3]

    c = pl.program_id(2)
    nc = pl.num_programs(2)

    @pl.when(c == 0)
    def _():
        if has_init:
            c_scr[...] = c0_ref[...]
            n_scr[...] = n0_ref[...]
            m_scr[...] = m0_ref[...]
        else:
            c_scr[...] = jnp.zeros(c_scr.shape, F32)
            n_scr[...] = jnp.zeros(n_scr.shape, F32)
            m_scr[...] = jnp.zeros(m_scr.shape, F32)

    tri = tri_ref[...]
    tri_t = trit_ref[...]
    keep = tri > 0.0
    gc = gc_ref[...]
    gr = gr_ref[...]
    dh = MLSTM_HEAD_DIM
    for hh in range(MLSTM_HEADS):
        sl = slice(hh * dh, (hh + 1) * dh)
        i_col = gc[:, hh:hh + 1]
        f_col = gc[:, MLSTM_HEADS + hh:MLSTM_HEADS + hh + 1]
        i_row = gr[hh:hh + 1, :]
        f_row = gr[MLSTM_HEADS + hh:MLSTM_HEADS + hh + 1, :]
        b_col = jnp.sum(tri * f_row, axis=1, keepdims=True)
        b_row = jnp.sum(tri_t * f_col, axis=0, keepdims=True)
        f_tot = jnp.sum(f_row, axis=1, keepdims=True)
        m_prev = m_scr[hh][:, 0:1]
        q_h = q_ref[:, sl]
        k_h = k_ref[:, sl] * (dh ** -0.5)
        v_h = v_ref[:, sl]
        dlog = jnp.where(keep, b_col - b_row + i_row, NEG)
        m_intra = jnp.max(dlog, axis=1, keepdims=True)
        inter = b_col + m_prev
        m_t = jnp.maximum(inter, m_intra)
        qk = lax.dot_general(q_h, k_h, (((1,), (1,)), ((), ())), preferred_element_type=F32)
        s = qk * jnp.exp(dlog - m_t)
        w_int = jnp.exp(inter - m_t)
        q_f = q_h.astype(F32)
        c_prev = c_scr[hh]
        n_prev = n_scr[hh]
        lhs = jnp.concatenate([s.astype(BF16), (q_f * w_int).astype(BF16)], axis=1)
        rhs = jnp.concatenate([v_h, c_prev.astype(BF16)], axis=0)
        num = jnp.dot(lhs, rhs, preferred_element_type=F32)
        den = jnp.sum(s, axis=1, keepdims=True) + w_int * jnp.sum(q_f * n_prev, axis=1, keepdims=True)
        hm_ref[:, sl] = num / jnp.maximum(jnp.abs(den), jnp.exp(-m_t))
        g_col = f_tot - b_col + i_col
        m_loc = jnp.max(g_col, axis=0, keepdims=True)
        kw = k_h.astype(F32) * jnp.exp(g_col - m_loc)
        c_loc = lax.dot_general(kw.astype(BF16), v_h, (((0,), (0,)), ((), ())), preferred_element_type=F32)
        n_loc = jnp.sum(kw, axis=0, keepdims=True)
        m_new = jnp.maximum(f_tot + m_prev, m_loc)
        wa = jnp.exp(f_tot + m_prev - m_new)
        wb = jnp.exp(m_loc - m_new)
        c_scr[hh] = wa * c_prev + wb * c_loc
        n_scr[hh] = wa * n_prev + wb * n_loc
        m_scr[hh] = jnp.broadcast_to(m_new, (1, 128))

    if emit_state:
        @pl.when(c == nc - 1)
        def _():
            cfin_ref[...] = c_scr[...]
            nfin_ref[...] = n_scr[...]
            mfin_ref[...] = m_scr[...]


def _mlstm(p_big, gcol, grow, tri, init, cbase, nseq, nc, emit_state):
    ci = _chunk_index(cbase, nc)
    co = _chunk_index(0, nc)
    has_init = init is not None
    dh, nh = MLSTM_HEAD_DIM, MLSTM_HEADS
    in_specs = [pl.BlockSpec((CHUNK, D_MODEL), lambda d, b, c: (ci(d, b, c), 3)),
                pl.BlockSpec((CHUNK, D_MODEL), lambda d, b, c: (ci(d, b, c), 4)),
                pl.BlockSpec((CHUNK, D_MODEL), lambda d, b, c: (ci(d, b, c), 5)),
                pl.BlockSpec((None, CHUNK, 2 * nh), lambda d, b, c: (d, ci(d, b, c), 0)),
                pl.BlockSpec((None, None, 2 * nh, CHUNK), lambda d, b, c: (d, ci(d, b, c), 0, 0)),
                pl.BlockSpec((None, CHUNK, CHUNK), lambda d, b, c: (d, 0, 0)),
                pl.BlockSpec((None, CHUNK, CHUNK), lambda d, b, c: (1 - d, 0, 0))]
    args = [p_big, p_big, p_big, gcol, grow, tri, tri]
    if has_init:
        in_specs += [pl.BlockSpec((None, None, nh, dh, dh), lambda d, b, c: (b, d, 0, 0, 0)),
                     pl.BlockSpec((None, None, nh, 1, dh), lambda d, b, c: (b, d, 0, 0, 0)),
                     pl.BlockSpec((None, None, nh, 1, 128), lambda d, b, c: (b, d, 0, 0, 0))]
        args += list(init)
    out_specs = [pl.BlockSpec((None, CHUNK, D_MODEL), lambda d, b, c: (d, co(d, b, c), 0))]
    out_shape = [jax.ShapeDtypeStruct((2, nseq * nc * CHUNK, D_MODEL), F32)]
    if emit_state:
        out_specs += [pl.BlockSpec((None, None, nh, dh, dh), lambda d, b, c: (b, d, 0, 0, 0)),
                      pl.BlockSpec((None, None, nh, 1, dh), lambda d, b, c: (b, d, 0, 0, 0)),
                      pl.BlockSpec((None, None, nh, 1, 128), lambda d, b, c: (b, d, 0, 0, 0))]
        out_shape += [jax.ShapeDtypeStruct((nseq, 2, nh, dh, dh), F32),
                      jax.ShapeDtypeStruct((nseq, 2, nh, 1, dh), F32),
                      jax.ShapeDtypeStruct((nseq, 2, nh, 1, 128), F32)]
    return pl.pallas_call(
        functools.partial(_mlstm_body, has_init, emit_state),
        grid=(2, nseq, nc),
        in_specs=in_specs,
        out_specs=out_specs,
        out_shape=out_shape,
        scratch_shapes=[pltpu.VMEM((nh, dh, dh), F32), pltpu.VMEM((nh, 1, dh), F32), pltpu.VMEM((nh, 1, 128), F32)],
        compiler_params=_cparams(("arbitrary", "arbitrary", "arbitrary")),
        name="mlstm_state" if emit_state else "mlstm_seeded",
    )(*args)


MG_TM = 512


def _merge_body(y_ref, hm_ref, z_ref, xs_ref, o_ref, g1_ref, g2_ref, x_ref, gta_ref, shf_ref, scf_ref,
                wso_ref, wmo_ref, wout_ref, gssd_ref, gml_ref, gpost_ref, gpre_ref, dskip_ref,
                x1_ref, h2_ref, h2b_ref):
    y = y_ref[0] + y_ref[1] + dskip_ref[...] * xs_ref[...].astype(F32)
    y = _rms(y * z_ref[...].astype(F32), gssd_ref[...])
    y_ssd = jnp.dot(y.astype(BF16), wso_ref[...], preferred_element_type=F32)
    hm = hm_ref[0] + hm_ref[1]
    gml = gml_ref[...]
    parts = []
    for hh in range(MLSTM_HEADS):
        sl = slice(hh * MLSTM_HEAD_DIM, (hh + 1) * MLSTM_HEAD_DIM)
        parts.append(_rms(hm[:, sl], gml[:, sl]))
    hm = jnp.concatenate(parts, axis=1) * o_ref[...].astype(F32)
    y_ml = jnp.dot(hm.astype(BF16), wmo_ref[...], preferred_element_type=F32)
    mixed = g1_ref[...].astype(F32) * y_ssd + g2_ref[...].astype(F32) * y_ml
    mixed = jnp.dot(mixed.astype(BF16), wout_ref[...], preferred_element_type=F32)
    x1 = x_ref[...] + gta_ref[...] * _rms(mixed, gpost_ref[...])
    x1_ref[...] = x1
    h2 = _rms(x1, gpre_ref[...]) * (1.0 + scf_ref[...]) + shf_ref[...]
    h2_ref[...] = h2
    h2b_ref[...] = h2.astype(BF16)


def _merge(yd, hmd, p_big, x, mod4, wso, wmo, wout, gssd, gml, gpost, gpre, dskip, row_of_tile):
    t = x.shape[0]
    tm = MG_TM
    tok = lambda col: pl.BlockSpec((tm, D_MODEL), lambda i: (i, col))
    modrow = lambda idx: pl.BlockSpec((None, None, 1, D_MODEL), lambda i: (row_of_tile(i), idx, 0, 0))
    vec = pl.BlockSpec((1, D_MODEL), lambda i: (0, 0))
    wspec = pl.BlockSpec((D_MODEL, D_MODEL), lambda i: (0, 0))
    return pl.pallas_call(
        _merge_body,
        grid=(t // tm,),
        in_specs=[pl.BlockSpec((2, tm, D_MODEL), lambda i: (0, i, 0)),
                  pl.BlockSpec((2, tm, D_MODEL), lambda i: (0, i, 0)),
                  tok(0), tok(1), tok(6), tok(7), tok(8), tok(0),
                  modrow(2), modrow(3), modrow(4),
                  wspec, wspec, wspec, vec, vec, vec, vec, vec],
        out_specs=[tok(0), tok(0), tok(0)],
        out_shape=[jax.ShapeDtypeStruct((t, D_MODEL), F32),
                   jax.ShapeDtypeStruct((t, D_MODEL), F32),
                   jax.ShapeDtypeStruct((t, D_MODEL), BF16)],
        compiler_params=_cparams(("parallel",)),
        name="merge",
    )(yd, hmd, p_big, p_big, p_big, p_big, p_big, x, mod4, mod4, mod4,
      wso, wmo, wout, gssd, gml, gpost, gpre, dskip)


RT_T = 256


def _top16(s, nrows):
    t = s.shape[1]
    rowid = lax.broadcasted_iota(jnp.int32, (nrows, t), 0).astype(F32)
    row16 = lax.broadcasted_iota(jnp.int32, (PEER_TOPK, t), 0)

    def body(j, carry):
        cur, rank, vals = carry
        m = jnp.max(cur, axis=0, keepdims=True)
        idx = jnp.min(jnp.where(cur == m, rowid, 1e9), axis=0, keepdims=True)
        sel = rowid == idx
        rank = jnp.where(sel, j.astype(F32), rank)
        cur = jnp.where(sel, -jnp.inf, cur)
        vals = jnp.where(row16 == j, m, vals)
        return cur, rank, vals

    init = (s, jnp.full((nrows, t), 127.0, F32), jnp.zeros((PEER_TOPK, t), F32))
    _, rank, vals = lax.fori_loop(0, PEER_TOPK, body, init)
    return rank, vals


def _route_body(h2_ref, wqt_ref, k1_ref, k2_ref, sel1_ref, sel2_ref, sel1t_ref,
                r2_ref, e2_ref, c1_ref, e1_ref):
    qt = lax.dot_general(wqt_ref[...], h2_ref[...], (((1,), (1,)), ((), ())), preferred_element_type=F32)
    k1 = k1_ref[...]
    k2 = k2_ref[...]
    half = PEER_N_KEYS
    for h in range(PEER_HEADS):
        q1 = qt[h * 2 * half:h * 2 * half + half, :]
        q2 = qt[h * 2 * half + half:(h + 1) * 2 * half, :]
        s1 = jnp.dot(k1, q1, precision=HIGHEST, preferred_element_type=F32)
        s2 = jnp.dot(k2, q2, precision=HIGHEST, preferred_element_type=F32)
        rank1, v1 = _top16(s1, PEER_N_KEYS)
        rank2, v2 = _top16(s2, PEER_N_KEYS)
        cand = (jnp.dot(sel1_ref[...], v1, precision=HIGHEST, preferred_element_type=F32)
                + jnp.dot(sel2_ref[...], v2, precision=HIGHEST, preferred_element_type=F32))
        crow = lax.broadcasted_iota(jnp.int32, cand.shape, 0)
        cand = jnp.where(crow < N_CAND, cand, -jnp.inf)
        crank, _ = _top16(cand, N_CAND_PAD)
        chosen = crank < float(PEER_TOPK)
        cmax = v1[0:1, :] + v2[0:1, :]
        z = jnp.sum(jnp.where(chosen, jnp.exp(cand - cmax), 0.0), axis=0, keepdims=True)
        cnt = jnp.dot(sel1t_ref[...], chosen.astype(F32), preferred_element_type=F32)
        c1 = jnp.zeros(s1.shape, F32)
        for j in range(PEER_TOPK):
            c1 = c1 + jnp.where(rank1 == float(j), cnt[j:j + 1, :], 0.0)
        r2_ref[h] = rank2.astype(BF16)
        e2_ref[h] = jnp.exp(s2 - v2[0:1, :]).astype(BF16)
        c1_ref[h] = c1
        e1_ref[h] = jnp.exp(s1 - v1[0:1, :]) / z


def _route(h2b, wqt, k1, k2, sel1, sel2, sel1t):
    t = h2b.shape[0]
    nk = PEER_N_KEYS
    full = lambda a: pl.BlockSpec(a.shape, lambda i: (0,) * a.ndim)
    ospec = pl.BlockSpec((PEER_HEADS, nk, RT_T), lambda i: (0, 0, i))
    return pl.pallas_call(
        _route_body,
        grid=(t // RT_T,),
        in_specs=[pl.BlockSpec((RT_T, D_MODEL), lambda i: (i, 0)),
                  full(wqt), full(k1), full(k2), full(sel1), full(sel2), full(sel1t)],
        out_specs=[ospec, ospec, ospec, ospec],
        out_shape=[jax.ShapeDtypeStruct((PEER_HEADS, nk, t), BF16),
                   jax.ShapeDtypeStruct((PEER_HEADS, nk, t), BF16),
                   jax.ShapeDtypeStruct((PEER_HEADS, nk, t), F32),
                   jax.ShapeDtypeStruct((PEER_HEADS, nk, t), F32)],
        compiler_params=_cparams(("parallel",)),
        name="route",
    )(h2b, wqt, k1, k2, sel1, sel2, sel1t)


PE_T = 512
PE_E = 1024
SQRT_HALF = float(np.sqrt(0.5))


def _peer_body(h2_ref, u_ref, vt_ref, r2_ref, e2_ref, c1_ref, e1_ref, x1_ref, gtf_ref, gpost_ref,
               o_ref, acc_scr, gt_scr):
    e = pl.program_id(1)
    ne = pl.num_programs(1)

    @pl.when(e == 0)
    def _():
        acc_scr[...] = jnp.zeros(acc_scr.shape, F32)

    h2 = h2_ref[...]
    nk = PEER_N_KEYS
    for ii in range(PE_E // nk):
        i1 = e * (PE_E // nk) + ii
        a = lax.dot_general(u_ref[ii * nk:(ii + 1) * nk, :], h2, (((1,), (1,)), ((), ())),
                            preferred_element_type=F32)
        act = 0.5 * a * (1.0 + lax.erf(a * SQRT_HALF))
        p = jnp.zeros((nk, PE_T), BF16)
        for h in range(PEER_HEADS):
            c1row = c1_ref[h, pl.ds(i1, 1), :].astype(BF16)
            e1row = e1_ref[h, pl.ds(i1, 1), :].astype(BF16)
            p = p + jnp.where(r2_ref[h] < c1row, e2_ref[h] * e1row, jnp.zeros((), BF16))
        gt_scr[ii * nk:(ii + 1) * nk, :] = (act * p.astype(F32)).astype(BF16)
    acc_scr[...] += jnp.dot(vt_ref[...], gt_scr[...], preferred_element_type=F32)

    @pl.when(e == ne - 1)
    def _():
        f = acc_scr[...].T
        o_ref[...] = x1_ref[...] + gtf_ref[...] * _rms(f, gpost_ref[...])


def _peer(h2b, u_b, vt_b, r2, e2, c1, e1, x1, mod4, gpost, row_of_tile):
    t = h2b.shape[0]
    nk = PEER_N_KEYS
    rspec = pl.BlockSpec((PEER_HEADS, nk, PE_T), lambda i, e: (0, 0, i))
    return pl.pallas_call(
        _peer_body,
        grid=(t // PE_T, PEER_EXPERTS // PE_E),
        in_specs=[pl.BlockSpec((PE_T, D_MODEL), lambda i, e: (i, 0)),
                  pl.BlockSpec((PE_E, D_MODEL), lambda i, e: (e, 0)),
                  pl.BlockSpec((D_MODEL, PE_E), lambda i, e: (0, e)),
                  rspec, rspec, rspec, rspec,
                  pl.BlockSpec((PE_T, D_MODEL), lambda i, e: (i, 0)),
                  pl.BlockSpec((None, None, 1, D_MODEL), lambda i, e: (row_of_tile(i), 5, 0, 0)),
                  pl.BlockSpec((1, D_MODEL), lambda i, e: (0, 0))],
        out_specs=pl.BlockSpec((PE_T, D_MODEL), lambda i, e: (i, 0)),
        out_shape=jax.ShapeDtypeStruct((t, D_MODEL), F32),
        scratch_shapes=[pltpu.VMEM((D_MODEL, PE_T), F32), pltpu.VMEM((PE_E, PE_T), BF16)],
        compiler_params=_cparams(("parallel", "arbitrary")),
        name="peer",
    )(h2b, u_b, vt_b, r2, e2, c1, e1, x1, mod4, gpost)


def _row_of_tile(tile_tokens, n_prompt_tokens, sample_len):
    n_prompt_tiles = n_prompt_tokens // tile_tokens
    tiles_per_seq = sample_len // tile_tokens

    def f(i):
        return jnp.where(i < n_prompt_tiles, 0, 1 + (i - n_prompt_tiles) // tiles_per_seq)
    return f


def kernel(x_prompt, x_sample, c, state_ssd, state_mlstm_c, state_mlstm_n, state_mlstm_m, c_ctx, w_mod, b_mod, g_mix_pre, g_mix_post, g_ffn_pre, g_ffn_post, w_in, conv_w, conv_b, ssd_a_log, ssd_dt_bias, ssd_d, ssd_norm_g, w_ssd_out, mlstm_i_bias, mlstm_f_bias, mlstm_norm_g, w_mlstm_out, w_out, peer_w_q, peer_keys1, peer_keys2, peer_u, peer_v):
    assert w_mod.shape[0] == 1, "single trunk layer"
    bp, lp_, d = x_prompt.shape
    bs, ls, _ = x_sample.shape
    assert d == D_MODEL and lp_ == SEQ and ls % IP_TM == 0 and (bp * lp_) % IP_TM == 0
    n_p, n_s = bp * lp_, bs * ls
    x = jnp.concatenate([x_prompt.reshape(n_p, d), x_sample.reshape(n_s, d)], axis=0)

    cvec = jnp.zeros((8, d), F32).at[0].set(c_ctx).at[1:1 + bs].set(c)
    mod = _adaln(cvec, w_mod[0], b_mod[0])
    mod4 = mod.reshape(8, N_MOD, 1, d)

    wi = w_in[0]
    o_z, o_xbc, o_dt = 0, SSD_WIDTH, SSD_WIDTH + 2 * D_MODEL
    o_q = o_dt + 2 * SSD_HEADS
    o_gate = o_q + 4 * D_MODEL
    o_merge = o_gate + 4 * MLSTM_HEADS
    w_big = jnp.concatenate([wi[:, o_z:o_dt], wi[:, o_q:o_gate], wi[:, o_merge:]], axis=1).astype(BF16)
    n_small = 2 * SSD_HEADS + 4 * MLSTM_HEADS
    w_small = jnp.concatenate([wi[:, o_dt:o_q], wi[:, o_gate:o_merge],
                               jnp.zeros((d, SMALL_COLS - n_small), F32)], axis=1)
    gate_bias = jnp.stack([mlstm_i_bias[0], mlstm_f_bias[0]], axis=1).reshape(-1)
    b_small = jnp.concatenate([ssd_dt_bias[0].reshape(-1), gate_bias,
                               jnp.zeros((SMALL_COLS - n_small,), F32)]).reshape(1, SMALL_COLS)
    cw = jnp.zeros((8, BIG_COLS), F32)
    cw = cw.at[0:3, o_xbc:o_dt].set(conv_w[0]).at[3, o_xbc:o_dt].set(conv_b[0])

    row_ip = _row_of_tile(IP_TM, n_p, ls)
    p_big, small = _inproj(x, mod4, g_mix_pre, w_big, w_small, b_small, cw, row_ip, n_p // IP_TM)

    lo = np.tril(np.ones((CHUNK, CHUNK), np.float32))
    tri = jnp.asarray(np.stack([lo, lo.T]))
    a_log = jnp.zeros((2, 1, SMALL_COLS), F32).at[:, 0, :SSD_HEADS].set(ssd_a_log[0])

    ncp, ncs = lp_ // CHUNK, ls // CHUNK
    y_p, st_ssd = _ssd(p_big, small, tri, a_log, None, 0, bp, ncp, True)
    (y_s,) = _ssd(p_big, small, tri, a_log, state_ssd[:, 0].reshape(bs, 2, SSD_WIDTH, SSD_STATE),
                  n_p // CHUNK, bs, ncs, False)
    yd = jnp.concatenate([y_p, y_s], axis=1)

    gates = small[:, 2 * SSD_HEADS:n_small].reshape(n_p + n_s, 2, 2 * MLSTM_HEADS)
    gcol = gates.transpose(1, 0, 2)
    grow = gcol.reshape(2, (n_p + n_s) // CHUNK, CHUNK, 2 * MLSTM_HEADS).transpose(0, 1, 3, 2)
    hm_p, st_c, st_n, st_m = _mlstm(p_big, gcol, grow, tri, None, 0, bp, ncp, True)
    init = (state_mlstm_c[:, 0],
            state_mlstm_n[:, 0].reshape(bs, 2, MLSTM_HEADS, 1, MLSTM_HEAD_DIM),
            jnp.broadcast_to(state_mlstm_m[:, 0][..., None, None], (bs, 2, MLSTM_HEADS, 1, 128)))
    (hm_s,) = _mlstm(p_big, gcol, grow, tri, init, n_p // CHUNK, bs, ncs, False)
    hmd = jnp.concatenate([hm_p, hm_s], axis=1)

    dskip = jnp.repeat(ssd_d[0, 0] + ssd_d[0, 1], SSD_HEAD_DIM).reshape(1, SSD_WIDTH)
    row_mg = _row_of_tile(MG_TM, n_p, ls)
    x1, h2, h2b = _merge(yd, hmd, p_big, x, mod4,
                         w_ssd_out[0].astype(BF16), w_mlstm_out[0].astype(BF16), w_out[0].astype(BF16),
                         ssd_norm_g, mlstm_norm_g, g_mix_post, g_ffn_pre, dskip, row_mg)
    del h2

    sel1 = np.zeros((N_CAND_PAD, PEER_TOPK), np.float32)
    sel2 = np.zeros((N_CAND_PAD, PEER_TOPK), np.float32)
    for r, (j, l) in enumerate(_CAND):
        sel1[r, j] = 1.0
        sel2[r, l] = 1.0
    r2, e2, c1, e1 = _route(h2b, peer_w_q[0].T.astype(BF16), peer_keys1[0], peer_keys2[0],
                            jnp.asarray(sel1), jnp.asarray(sel2), jnp.asarray(sel1.T.copy()))

    row_pe = _row_of_tile(PE_T, n_p, ls)
    x2 = _peer(h2b, peer_u[0].astype(BF16), peer_v[0].T.astype(BF16), r2, e2, c1, e1, x1, mod4,
               g_ffn_post, row_pe)

    y_prompt = x2[:n_p].reshape(bp, lp_, d)
    y_sample = x2[n_p:].reshape(bs, ls, d)
    new_ssd = st_ssd.reshape(bp, 1, 2, SSD_HEADS, SSD_HEAD_DIM, SSD_STATE)
    new_c = st_c.reshape(bp, 1, 2, MLSTM_HEADS, MLSTM_HEAD_DIM, MLSTM_HEAD_DIM)
    new_n = st_n.reshape(bp, 1, 2, MLSTM_HEADS, MLSTM_HEAD_DIM)
    new_m = st_m[..., 0, 0].reshape(bp, 1, 2, MLSTM_HEADS)
    return (y_prompt, y_sample, new_ssd, new_c, new_n, new_m)
```

```python
import functools

import numpy as np
import jax
import jax.numpy as jnp
from jax import lax
from jax.experimental import pallas as pl
from jax.experimental.pallas import tpu as pltpu

F32 = jnp.float32
BF16 = jnp.bfloat16
HIGHEST = lax.Precision.HIGHEST

D_MODEL = 1024
EPS = 1e-6
N_MOD = 6
CHUNK = 128
SEQ = 256
GRID_W = 64
SSD_HEADS = 16
SSD_HEAD_DIM = 64
SSD_STATE = 128
SSD_GROUPS = 4
SSD_WIDTH = SSD_HEADS * SSD_HEAD_DIM
MLSTM_HEADS = 4
MLSTM_HEAD_DIM = 256
PEER_HEADS = 8
PEER_N_KEYS = 128
PEER_TOPK = 16
PEER_EXPERTS = PEER_N_KEYS * PEER_N_KEYS
SMALL_COLS = 128
BIG_COLS = 9 * D_MODEL
NEG = -1e30

VMEM_LIMIT = 56 * 1024 * 1024

_CAND = [(j, l) for j in range(PEER_TOPK) for l in range(PEER_TOPK) if (j + 1) * (l + 1) <= PEER_TOPK]
N_CAND = len(_CAND)
N_CAND_PAD = 56


def _cparams(sem):
    return pltpu.CompilerParams(dimension_semantics=sem, vmem_limit_bytes=VMEM_LIMIT)


def _sigmoid(x):
    return 1.0 / (1.0 + jnp.exp(-x))


def _silu(x):
    return x * _sigmoid(x)


def _rms(x, g):
    return x * lax.rsqrt(jnp.mean(x * x, axis=-1, keepdims=True) + EPS) * g


def _adaln_body(c_ref, w_ref, b_ref, o_ref):
    cv = c_ref[...]
    o_ref[...] = jnp.dot(_silu(cv), w_ref[...], precision=HIGHEST, preferred_element_type=F32) + b_ref[...]


def _adaln(cvec, w_mod, b_mod):
    tn = 512
    n = w_mod.shape[1]
    return pl.pallas_call(
        _adaln_body,
        grid=(n // tn,),
        in_specs=[pl.BlockSpec((8, D_MODEL), lambda j: (0, 0)),
                  pl.BlockSpec((D_MODEL, tn), lambda j: (0, j)),
                  pl.BlockSpec((1, tn), lambda j: (0, j))],
        out_specs=pl.BlockSpec((8, tn), lambda j: (0, j)),
        out_shape=jax.ShapeDtypeStruct((8, n), F32),
        compiler_params=_cparams(("arbitrary",)),
        name="adaln",
    )(cvec, w_mod, b_mod.reshape(1, n))


IP_TM = 1024
IP_TN = 512


def _inproj_body(n_prompt_tiles, xp_ref, xs_ref, sh_ref, sc_ref, g_ref, w_ref, ws_ref, bs_ref, cw_ref,
                 p_ref, s_ref, h_scr):
    i = pl.program_id(0)
    j = pl.program_id(1)

    @pl.when(j == 0)
    def _():
        x = jnp.where(i < n_prompt_tiles, xp_ref[...], xs_ref[...])
        h = _rms(x, g_ref[...]) * (1.0 + sc_ref[...]) + sh_ref[...]
        h_scr[...] = h.astype(BF16)
        sm = jnp.dot(h, ws_ref[...], precision=HIGHEST, preferred_element_type=F32) + bs_ref[...]
        col = lax.broadcasted_iota(jnp.int32, sm.shape, 1)
        is_f = (col >= 32) & (col < 48) & ((((col - 32) // MLSTM_HEADS) % 2) == 1)
        lp = jnp.log1p(jnp.exp(-jnp.abs(sm)))
        softplus = jnp.maximum(sm, 0.0) + lp
        logsig = jnp.minimum(sm, 0.0) - lp
        s_ref[...] = jnp.where(col < 32, softplus, jnp.where(is_f, logsig, sm))

    acc = jnp.dot(h_scr[...], w_ref[...], preferred_element_type=F32)

    @pl.when(j < 2)
    def _():
        p_ref[...] = _silu(acc).astype(BF16)

    @pl.when((j >= 2) & (j < 6))
    def _():
        seg_mask = jnp.where(i < n_prompt_tiles, SEQ - 1, GRID_W - 1)
        pos = lax.broadcasted_iota(jnp.int32, acc.shape, 0) & seg_mask
        prev = jnp.where(pos == 0, 0.0, pltpu.roll(acc, 1, 0))
        nxt = jnp.where(pos == seg_mask, 0.0, pltpu.roll(acc, IP_TM - 1, 0))
        cw = cw_ref[...]
        u = cw[3:4] + cw[0:1] * prev + cw[1:2] * acc + cw[2:3] * nxt
        p_ref[...] = _silu(u).astype(BF16)

    @pl.when((j >= 6) & (j < 12))
    def _():
        p_ref[...] = acc.astype(BF16)

    @pl.when(j >= 12)
    def _():
        p_ref[...] = _sigmoid(acc).astype(BF16)


def _split_specs(tm, npt):
    return [pl.BlockSpec((tm, D_MODEL), lambda i, *_: (jnp.minimum(i, npt - 1), 0)),
            pl.BlockSpec((tm, D_MODEL), lambda i, *_: (jnp.maximum(i - npt, 0), 0))]


def _inproj(xp, xs, mod4, g, w_big, w_small, b_small, cw, row_of_tile, n_prompt_tiles):
    t = xp.shape[0] + xs.shape[0]
    grid = (t // IP_TM, BIG_COLS // IP_TN)
    return pl.pallas_call(
        functools.partial(_inproj_body, n_prompt_tiles),
        grid=grid,
        in_specs=_split_specs(IP_TM, n_prompt_tiles) + [
                  pl.BlockSpec((None, None, 1, D_MODEL), lambda i, j: (row_of_tile(i), 0, 0, 0)),
                  pl.BlockSpec((None, None, 1, D_MODEL), lambda i, j: (row_of_tile(i), 1, 0, 0)),
                  pl.BlockSpec((1, D_MODEL), lambda i, j: (0, 0)),
                  pl.BlockSpec((D_MODEL, IP_TN), lambda i, j: (0, j)),
                  pl.BlockSpec((D_MODEL, SMALL_COLS), lambda i, j: (0, 0)),
                  pl.BlockSpec((1, SMALL_COLS), lambda i, j: (0, 0)),
                  pl.BlockSpec((8, IP_TN), lambda i, j: (0, j))],
        out_specs=[pl.BlockSpec((IP_TM, IP_TN), lambda i, j: (i, j)),
                   pl.BlockSpec((IP_TM, SMALL_COLS), lambda i, j: (i, 0))],
        out_shape=[jax.ShapeDtypeStruct((t, BIG_COLS), BF16),
                   jax.ShapeDtypeStruct((t, SMALL_COLS), F32)],
        scratch_shapes=[pltpu.VMEM((IP_TM, D_MODEL), BF16)],
        compiler_params=_cparams(("parallel", "arbitrary")),
        name="inproj",
    )(xp, xs, mod4, mod4, g, w_big, w_small, b_small, cw)


def _ssd_body(has_h0, emit_state, *refs):
    refs = list(refs)
    xs_ref, bc_ref, sm_ref, tri_ref, alog_ref, exp_ref = refs[:6]
    k = 6
    h0_ref = None
    if has_h0:
        h0_ref = refs[k]
        k += 1
    y_ref = refs[k]
    k += 1
    hfin_ref = None
    if emit_state:
        hfin_ref = refs[k]
        k += 1
    hs_scr = refs[k]

    d = pl.program_id(0)
    c = pl.program_id(2)
    nc = pl.num_programs(2)

    @pl.when(c == 0)
    def _():
        if has_h0:
            hs_scr[...] = h0_ref[...].T
        else:
            hs_scr[...] = jnp.zeros(hs_scr.shape, F32)

    tri = tri_ref[...]
    keep = tri > 0.0
    sm = sm_ref[...]
    lane = lax.broadcasted_iota(jnp.int32, sm.shape, 1)
    dt = jnp.where(d == 0, sm, pltpu.roll(sm, SMALL_COLS - SSD_HEADS, 1))
    a_row = -jnp.exp(alog_ref[...])
    dta = jnp.where(lane < SSD_HEADS, dt * a_row, 0.0)
    cum = jnp.dot(tri, dta, precision=HIGHEST, preferred_element_type=F32)
    cum_t = cum.T
    tot = jnp.where(d == 0, cum[CHUNK - 1:CHUNK, :], cum[0:1, :])
    per_head = jnp.concatenate([jnp.where(lane < SSD_HEADS, dt, 0.0), jnp.exp(cum), jnp.exp(tot - cum),
                                jnp.broadcast_to(jnp.exp(tot), (8, SMALL_COLS))], axis=0)
    hi = per_head.astype(BF16)
    lo = (per_head - hi.astype(F32)).astype(BF16)
    expand = exp_ref[...]
    wide = (jnp.dot(hi, expand, preferred_element_type=F32)
            + jnp.dot(lo, expand, preferred_element_type=F32))
    dt_w = wide[0:CHUNK]
    decay_in_w = wide[CHUNK:2 * CHUNK]
    decay_out_w = wide[2 * CHUNK:3 * CHUNK]
    decay_tot_w = wide[3 * CHUNK:3 * CHUNK + 1]
    xdt = xs_ref[...].astype(F32) * dt_w
    xdt_b = xdt.astype(BF16)
    xdec_b = (xdt * decay_out_w).astype(BF16)
    low_half = lax.broadcasted_iota(jnp.int32, (CHUNK, 2 * SSD_HEAD_DIM), 1) < SSD_HEAD_DIM
    zero_b = jnp.zeros((CHUNK, 2 * SSD_HEAD_DIM), BF16)

    bc = bc_ref[...]
    hpg = SSD_HEADS // SSD_GROUPS
    gw = hpg * SSD_HEAD_DIM
    for g in range(SSD_GROUPS):
        b_g = bc[:, g * SSD_STATE:(g + 1) * SSD_STATE]
        c_g = bc[:, (SSD_GROUPS + g) * SSD_STATE:(SSD_GROUPS + g + 1) * SSD_STATE]
        cb = lax.dot_general(c_g, b_g, (((1,), (1,)), ((), ())), preferred_element_type=F32)
        gs = slice(g * gw, (g + 1) * gw)
        hs_g = hs_scr[:, gs]
        y_off = jnp.dot(c_g, hs_g.astype(BF16), preferred_element_type=F32) * decay_in_w[:, gs]
        for pr in range(hpg // 2):
            h0 = g * hpg + 2 * pr
            ps = slice(h0 * SSD_HEAD_DIM, (h0 + 2) * SSD_HEAD_DIM)
            s_pair = []
            for h in (h0, h0 + 1):
                lmat = jnp.exp(jnp.where(keep, cum[:, h:h + 1] - cum_t[h:h + 1, :], NEG))
                s_pair.append((cb * lmat).astype(BF16))
            x_pair = xdt_b[:, ps]
            rhs = jnp.concatenate([jnp.where(low_half, x_pair, zero_b),
                                   jnp.where(low_half, zero_b, x_pair)], axis=0)
            y_ref[:, ps] = (jnp.dot(jnp.concatenate(s_pair, axis=1), rhs, preferred_element_type=F32)
                            + y_off[:, 2 * pr * SSD_HEAD_DIM:(2 * pr + 2) * SSD_HEAD_DIM])
        upd = lax.dot_general(b_g, xdec_b[:, gs], (((0,), (0,)), ((), ())), preferred_element_type=F32)
        hs_scr[:, gs] = decay_tot_w[:, gs] * hs_g + upd

    if emit_state:
        @pl.when(c == nc - 1)
        def _():
            hfin_ref[...] = hs_scr[...].T


def _chunk_index(cbase, nc):
    def f(d, b, c):
        return cbase + b * nc + c + d * (nc - 1 - 2 * c)
    return f


def _ssd(p_big, small, tri, a_log, h0, cbase, nseq, nc, emit_state):
    ci = _chunk_index(cbase, nc)
    co = _chunk_index(0, nc)
    has_h0 = h0 is not None
    in_specs = [pl.BlockSpec((CHUNK, D_MODEL), lambda d, b, c: (ci(d, b, c), 1)),
                pl.BlockSpec((CHUNK, D_MODEL), lambda d, b, c: (ci(d, b, c), 2)),
                pl.BlockSpec((CHUNK, SMALL_COLS), lambda d, b, c: (ci(d, b, c), 0)),
                pl.BlockSpec((None, CHUNK, CHUNK), lambda d, b, c: (d, 0, 0)),
                pl.BlockSpec((None, 1, SMALL_COLS), lambda d, b, c: (d, 0, 0)),
                pl.BlockSpec((SMALL_COLS, SSD_WIDTH), lambda d, b, c: (0, 0))]
    head_of_lane = np.arange(SSD_WIDTH) // SSD_HEAD_DIM
    expand = jnp.asarray(np.arange(SMALL_COLS)[:, None] == head_of_lane[None, :], BF16)
    args = [p_big, p_big, small, tri, a_log, expand]
    if has_h0:
        in_specs.append(pl.BlockSpec((None, None, SSD_WIDTH, SSD_STATE), lambda d, b, c: (b, d, 0, 0)))
        args.append(h0)
    out_specs = [pl.BlockSpec((None, CHUNK, SSD_WIDTH), lambda d, b, c: (d, co(d, b, c), 0))]
    out_shape = [jax.ShapeDtypeStruct((2, nseq * nc * CHUNK, SSD_WIDTH), F32)]
    if emit_state:
        out_specs.append(pl.BlockSpec((None, None, SSD_WIDTH, SSD_STATE), lambda d, b, c: (b, d, 0, 0)))
        out_shape.append(jax.ShapeDtypeStruct((nseq, 2, SSD_WIDTH, SSD_STATE), F32))
    return pl.pallas_call(
        functools.partial(_ssd_body, has_h0, emit_state),
        grid=(2, nseq, nc),
        in_specs=in_specs,
        out_specs=out_specs,
        out_shape=out_shape,
        scratch_shapes=[pltpu.VMEM((SSD_STATE, SSD_WIDTH), F32)],
        compiler_params=_cparams(("arbitrary", "arbitrary", "arbitrary")),
        name="ssd_state" if emit_state else "ssd_seeded",
    )(*args)


def _mlstm_body(has_init, emit_state, *refs):
    refs = list(refs)
    q_ref, k_ref, v_ref, gc_ref, gr_ref, tri_ref, trit_ref = refs[:7]
    k_i = 7
    c0_ref = n0_ref = m0_ref = None
    if has_init:
        c0_ref, n0_ref, m0_ref = refs[k_i:k_i + 3]
        k_i += 3
    hm_ref = refs[k_i]
    k_i += 1
    cfin_ref = nfin_ref = mfin_ref = None
    if emit_state:
        cfin_ref, nfin_ref, mfin_ref = refs[k_i:k_i + 3]
        k_i += 3
    c_scr, n_scr, m_scr = refs[k_i:k_i + 3]

    c = pl.program_id(2)
    nc = pl.num_programs(2)

    @pl.when(c == 0)
    def _():
        if has_init:
            c_scr[...] = c0_ref[...]
            n_scr[...] = n0_ref[...]
            m_scr[...] = m0_ref[...]
        else:
            c_scr[...] = jnp.zeros(c_scr.shape, F32)
            n_scr[...] = jnp.zeros(n_scr.shape, F32)
            m_scr[...] = jnp.zeros(m_scr.shape, F32)

    tri = tri_ref[...]
    tri_t = trit_ref[...]
    keep = tri > 0.0
    gc = gc_ref[...]
    gr = gr_ref[...]
    dh = MLSTM_HEAD_DIM
    for hh in range(MLSTM_HEADS):
        sl = slice(hh * dh, (hh + 1) * dh)
        i_col = gc[:, hh:hh + 1]
        f_col = gc[:, MLSTM_HEADS + hh:MLSTM_HEADS + hh + 1]
        i_row = gr[hh:hh + 1, :]
        f_row = gr[MLSTM_HEADS + hh:MLSTM_HEADS + hh + 1, :]
        b_col = jnp.sum(tri * f_row, axis=1, keepdims=True)
        b_row = jnp.sum(tri_t * f_col, axis=0, keepdims=True)
        f_tot = jnp.sum(f_row, axis=1, keepdims=True)
        m_prev = m_scr[hh][:, 0:1]
        q_h = q_ref[:, sl]
        k_h = k_ref[:, sl] * (dh ** -0.5)
        v_h = v_ref[:, sl]
        dlog = jnp.where(keep, b_col - b_row + i_row, NEG)
        m_intra = jnp.max(dlog, axis=1, keepdims=True)
        inter = b_col + m_prev
        m_t = jnp.maximum(inter, m_intra)
        qk = lax.dot_general(q_h, k_h, (((1,), (1,)), ((), ())), preferred_element_type=F32)
        s = qk * jnp.exp(dlog - m_t)
        w_int = jnp.exp(inter - m_t)
        q_f = q_h.astype(F32)
        c_prev = c_scr[hh]
        n_prev = n_scr[hh]
        lhs = jnp.concatenate([s.astype(BF16), (q_f * w_int).astype(BF16)], axis=1)
        rhs = jnp.concatenate([v_h, c_prev.astype(BF16)], axis=0)
        num = jnp.dot(lhs, rhs, preferred_element_type=F32)
        den = jnp.sum(s, axis=1, keepdims=True) + w_int * jnp.sum(q_f * n_prev, axis=1, keepdims=True)
        hm_ref[:, sl] = num / jnp.maximum(jnp.abs(den), jnp.exp(-m_t))
        g_col = f_tot - b_col + i_col
        m_loc = jnp.max(g_col, axis=0, keepdims=True)
        kw = k_h.astype(F32) * jnp.exp(g_col - m_loc)
        c_loc = lax.dot_general(kw.astype(BF16), v_h, (((0,), (0,)), ((), ())), preferred_element_type=F32)
        n_loc = jnp.sum(kw, axis=0, keepdims=True)
        m_new = jnp.maximum(f_tot + m_prev, m_loc)
        wa = jnp.exp(f_tot + m_prev - m_new)
        wb = jnp.exp(m_loc - m_new)
        c_scr[hh] = wa * c_prev + wb * c_loc
        n_scr[hh] = wa * n_prev + wb * n_loc
        m_scr[hh] = jnp.broadcast_to(m_new, (1, 128))

    if emit_state:
        @pl.when(c == nc - 1)
        def _():
            cfin_ref[...] = c_scr[...]
            nfin_ref[...] = n_scr[...]
            mfin_ref[...] = m_scr[...]


def _mlstm(p_big, gcol, grow, tri, init, cbase, nseq, nc, emit_state):
    ci = _chunk_index(cbase, nc)
    co = _chunk_index(0, nc)
    has_init = init is not None
    dh, nh = MLSTM_HEAD_DIM, MLSTM_HEADS
    in_specs = [pl.BlockSpec((CHUNK, D_MODEL), lambda d, b, c: (ci(d, b, c), 3)),
                pl.BlockSpec((CHUNK, D_MODEL), lambda d, b, c: (ci(d, b, c), 4)),
                pl.BlockSpec((CHUNK, D_MODEL), lambda d, b, c: (ci(d, b, c), 5)),
                pl.BlockSpec((None, CHUNK, 2 * nh), lambda d, b, c: (d, ci(d, b, c), 0)),
                pl.BlockSpec((None, None, 2 * nh, CHUNK), lambda d, b, c: (d, ci(d, b, c), 0, 0)),
                pl.BlockSpec((None, CHUNK, CHUNK), lambda d, b, c: (d, 0, 0)),
                pl.BlockSpec((None, CHUNK, CHUNK), lambda d, b, c: (1 - d, 0, 0))]
    args = [p_big, p_big, p_big, gcol, grow, tri, tri]
    if has_init:
        in_specs += [pl.BlockSpec((None, None, nh, dh, dh), lambda d, b, c: (b, d, 0, 0, 0)),
                     pl.BlockSpec((None, None, nh, 1, dh), lambda d, b, c: (b, d, 0, 0, 0)),
                     pl.BlockSpec((None, None, nh, 1, 128), lambda d, b, c: (b, d, 0, 0, 0))]
        args += list(init)
    out_specs = [pl.BlockSpec((None, CHUNK, D_MODEL), lambda d, b, c: (d, co(d, b, c), 0))]
    out_shape = [jax.ShapeDtypeStruct((2, nseq * nc * CHUNK, D_MODEL), F32)]
    if emit_state:
        out_specs += [pl.BlockSpec((None, None, nh, dh, dh), lambda d, b, c: (b, d, 0, 0, 0)),
                      pl.BlockSpec((None, None, nh, 1, dh), lambda d, b, c: (b, d, 0, 0, 0)),
                      pl.BlockSpec((None, None, nh, 1, 128), lambda d, b, c: (b, d, 0, 0, 0))]
        out_shape += [jax.ShapeDtypeStruct((nseq, 2, nh, dh, dh), F32),
                      jax.ShapeDtypeStruct((nseq, 2, nh, 1, dh), F32),
                      jax.ShapeDtypeStruct((nseq, 2, nh, 1, 128), F32)]
    return pl.pallas_call(
        functools.partial(_mlstm_body, has_init, emit_state),
        grid=(2, nseq, nc),
        in_specs=in_specs,
        out_specs=out_specs,
        out_shape=out_shape,
        scratch_shapes=[pltpu.VMEM((nh, dh, dh), F32), pltpu.VMEM((nh, 1, dh), F32), pltpu.VMEM((nh, 1, 128), F32)],
        compiler_params=_cparams(("arbitrary", "arbitrary", "arbitrary")),
        name="mlstm_state" if emit_state else "mlstm_seeded",
    )(*args)


MG_TM = 256


def _merge_body(n_prompt_tiles, yp_ref, ys_ref, hmp_ref, hms_ref, xp_ref, xsm_ref,
                z_ref, xs_ref, o_ref, g1_ref, g2_ref, gta_ref, shf_ref, scf_ref,
                wso_ref, wmo_ref, wout_ref, gssd_ref, gml_ref, gpost_ref, gpre_ref, dskip_ref,
                x1_ref, h2b_ref):
    is_prompt = pl.program_id(0) < n_prompt_tiles
    y = jnp.where(is_prompt, yp_ref[0] + yp_ref[1], ys_ref[0] + ys_ref[1])
    y = y + dskip_ref[...] * xs_ref[...].astype(F32)
    y = _rms(y * z_ref[...].astype(F32), gssd_ref[...])
    y_ssd = jnp.dot(y.astype(BF16), wso_ref[...], preferred_element_type=F32)
    hm = jnp.where(is_prompt, hmp_ref[0] + hmp_ref[1], hms_ref[0] + hms_ref[1])
    gml = gml_ref[...]
    parts = []
    for hh in range(MLSTM_HEADS):
        sl = slice(hh * MLSTM_HEAD_DIM, (hh + 1) * MLSTM_HEAD_DIM)
        parts.append(_rms(hm[:, sl], gml[:, sl]))
    hm = jnp.concatenate(parts, axis=1) * o_ref[...].astype(F32)
    y_ml = jnp.dot(hm.astype(BF16), wmo_ref[...], preferred_element_type=F32)
    mixed = g1_ref[...].astype(F32) * y_ssd + g2_ref[...].astype(F32) * y_ml
    mixed = jnp.dot(mixed.astype(BF16), wout_ref[...], preferred_element_type=F32)
    x = jnp.where(is_prompt, xp_ref[...], xsm_ref[...])
    x1 = x + gta_ref[...] * _rms(mixed, gpost_ref[...])
    x1_ref[...] = x1
    h2 = _rms(x1, gpre_ref[...]) * (1.0 + scf_ref[...]) + shf_ref[...]
    h2b_ref[...] = h2.astype(BF16)


def _merge(y_p, y_s, hm_p, hm_s, xp, xs, p_big, mod4, wso, wmo, wout, gssd, gml, gpost, gpre, dskip,
           row_of_tile):
    t = xp.shape[0] + xs.shape[0]
    tm = MG_TM
    npt = xp.shape[0] // tm
    tok = lambda col: pl.BlockSpec((tm, D_MODEL), lambda i: (i, col))
    both_p = pl.BlockSpec((2, tm, D_MODEL), lambda i: (0, jnp.minimum(i, npt - 1), 0))
    both_s = pl.BlockSpec((2, tm, D_MODEL), lambda i: (0, jnp.maximum(i - npt, 0), 0))
    modrow = lambda idx: pl.BlockSpec((None, None, 1, D_MODEL), lambda i: (row_of_tile(i), idx, 0, 0))
    vec = pl.BlockSpec((1, D_MODEL), lambda i: (0, 0))
    wspec = pl.BlockSpec((D_MODEL, D_MODEL), lambda i: (0, 0))
    return pl.pallas_call(
        functools.partial(_merge_body, npt),
        grid=(t // tm,),
        in_specs=[both_p, both_s, both_p, both_s] + _split_specs(tm, npt) + [
                  tok(0), tok(1), tok(6), tok(7), tok(8),
                  modrow(2), modrow(3), modrow(4),
                  wspec, wspec, wspec, vec, vec, vec, vec, vec],
        out_specs=[tok(0), tok(0)],
        out_shape=[jax.ShapeDtypeStruct((t, D_MODEL), F32),
                   jax.ShapeDtypeStruct((t, D_MODEL), BF16)],
        compiler_params=_cparams(("parallel",)),
        name="merge",
    )(y_p, y_s, hm_p, hm_s, xp, xs, p_big, p_big, p_big, p_big, p_big, mod4, mod4, mod4,
      wso, wmo, wout, gssd, gml, gpost, gpre, dskip)


RT_T = 256


def _top16(s, nrows):
    t = s.shape[1]
    rowid = lax.broadcasted_iota(jnp.int32, (nrows, t), 0).astype(F32)
    row16 = lax.broadcasted_iota(jnp.int32, (PEER_TOPK, t), 0)

    def body(j, carry):
        cur, rank, vals = carry
        m = jnp.max(cur, axis=0, keepdims=True)
        idx = jnp.min(jnp.where(cur == m, rowid, 1e9), axis=0, keepdims=True)
        sel = rowid == idx
        rank = jnp.where(sel, j.astype(F32), rank)
        cur = jnp.where(sel, -jnp.inf, cur)
        vals = jnp.where(row16 == j, m, vals)
        return cur, rank, vals

    init = (s, jnp.full((nrows, t), 127.0, F32), jnp.zeros((PEER_TOPK, t), F32))
    _, rank, vals = lax.fori_loop(0, PEER_TOPK, body, init)
    return rank, vals


MARK = 2.0 ** 100


def _top16_pair_marked(sa, sb):
    lanes = sa.shape[1]
    row16 = lax.broadcasted_iota(jnp.int32, (PEER_TOPK, lanes), 0)

    def body(j, carry):
        ca, cb, va, vb = carry
        mk = -(j.astype(F32) + 1.0) * MARK
        ma = jnp.max(ca, axis=0, keepdims=True)
        mb = jnp.max(cb, axis=0, keepdims=True)
        ca = jnp.where(ca == ma, mk, ca)
        cb = jnp.where(cb == mb, mk, cb)
        va = jnp.where(row16 == j, ma, va)
        vb = jnp.where(row16 == j, mb, vb)
        return ca, cb, va, vb

    zeros = jnp.zeros((PEER_TOPK, lanes), F32)
    return lax.fori_loop(0, PEER_TOPK, body, (sa, sb, zeros, zeros))


def _marked_rank(s, cur):
    took = cur <= -MARK
    rank = jnp.where(took, cur * (-1.0 / MARK) - 1.0, 127.0)
    n_took = jnp.sum(took.astype(F32), axis=0, keepdims=True)
    bad = (n_took != float(PEER_TOPK)) | (jnp.min(s, axis=0, keepdims=True) <= -0.25 * MARK)
    return rank, bad.astype(F32)


def _chosen_marked(cand):
    def body(j, cur):
        m = jnp.max(cur, axis=0, keepdims=True)
        return jnp.where(cur == m, -2.0 * MARK, cur)

    cur = lax.fori_loop(0, PEER_TOPK, body, cand)
    took = cur <= -MARK
    n_took = jnp.sum(took.astype(F32), axis=0, keepdims=True)
    return took, (n_took != float(PEER_TOPK)).astype(F32)


def _route_tail(marked, h, s1, s2, rank1, rank2, v1, v2,
                sel1_ref, sel2_ref, sel1t_ref, r2_ref, e2_ref, c1_ref, e1_ref):
    cand = (jnp.dot(sel1_ref[...], v1, precision=HIGHEST, preferred_element_type=F32)
            + jnp.dot(sel2_ref[...], v2, precision=HIGHEST, preferred_element_type=F32))
    crow = lax.broadcasted_iota(jnp.int32, cand.shape, 0)
    if marked:
        chosen, bad = _chosen_marked(jnp.where(crow < N_CAND, cand, -0.75 * MARK))
    else:
        cand = jnp.where(crow < N_CAND, cand, -jnp.inf)
        crank, _ = _top16(cand, N_CAND_PAD)
        chosen = crank < float(PEER_TOPK)
        bad = jnp.zeros((1, cand.shape[1]), F32)
    cmax = v1[0:1, :] + v2[0:1, :]
    z = jnp.sum(jnp.where(chosen, jnp.exp(cand - cmax), 0.0), axis=0, keepdims=True)
    cnt = jnp.dot(sel1t_ref[...], chosen.astype(F32), preferred_element_type=F32)
    c1 = jnp.zeros(s1.shape, F32)
    for j in range(PEER_TOPK):
        c1 = jnp.where(rank1 == float(j), cnt[j:j + 1, :], c1)
    r2_ref[h] = rank2.astype(BF16)
    e2_ref[h] = jnp.exp(s2 - v2[0:1, :]).astype(BF16)
    c1_ref[h] = c1
    e1_ref[h] = jnp.exp(s1 - v1[0:1, :]) / z
    return bad


def _route_body(h2_ref, m_ref, sel1_ref, sel2_ref, sel1t_ref, r2_ref, e2_ref, c1_ref, e1_ref, sc_scr):
    sc_scr[...] = lax.dot_general(m_ref[...], h2_ref[...], (((1,), (1,)), ((), ())), preferred_element_type=F32)
    nk = PEER_N_KEYS
    tail_refs = (sel1_ref, sel2_ref, sel1t_ref, r2_ref, e2_ref, c1_ref, e1_ref)
    for h in range(PEER_HEADS):
        s1 = sc_scr[h * 2 * nk:h * 2 * nk + nk, :]
        s2 = sc_scr[h * 2 * nk + nk:(h + 1) * 2 * nk, :]
        ranks1, ranks2, vals1, vals2 = [], [], [], []
        n_bad = jnp.zeros((1, 128), F32)
        for sub in range(RT_T // 128):
            ls = slice(sub * 128, (sub + 1) * 128)
            ca, cb, va, vb = _top16_pair_marked(s1[:, ls], s2[:, ls])
            ra, bad_a = _marked_rank(s1[:, ls], ca)
            rb, bad_b = _marked_rank(s2[:, ls], cb)
            n_bad = n_bad + bad_a + bad_b
            ranks1.append(ra)
            ranks2.append(rb)
            vals1.append(va)
            vals2.append(vb)
        cat = lambda parts: jnp.concatenate(parts, axis=1)
        bad_c = _route_tail(True, h, s1, s2, cat(ranks1), cat(ranks2), cat(vals1), cat(vals2), *tail_refs)

        @pl.when(jnp.sum(n_bad) + jnp.sum(bad_c) > 0.0)
        def _():
            rank1, v1 = _top16(s1, nk)
            rank2, v2 = _top16(s2, nk)
            _route_tail(False, h, s1, s2, rank1, rank2, v1, v2, *tail_refs)


def _fold_body(k_ref, wqt_ref, m_ref):
    m_ref[...] = jnp.dot(k_ref[...], wqt_ref[...], precision=HIGHEST, preferred_element_type=F32).astype(BF16)


def _fold_keys(keys, wqt):
    nk = PEER_N_KEYS
    nblk = wqt.shape[0] // nk
    return pl.pallas_call(
        _fold_body,
        grid=(nblk,),
        in_specs=[pl.BlockSpec((None, nk, nk), lambda r: (r % 2, 0, 0)),
                  pl.BlockSpec((nk, D_MODEL), lambda r: (r, 0))],
        out_specs=pl.BlockSpec((nk, D_MODEL), lambda r: (r, 0)),
        out_shape=jax.ShapeDtypeStruct(wqt.shape, BF16),
        compiler_params=_cparams(("parallel",)),
        name="fold_keys",
    )(keys, wqt)


def _route(h2b, m_fold, sel1, sel2, sel1t):
    t = h2b.shape[0]
    nk = PEER_N_KEYS
    full = lambda a: pl.BlockSpec(a.shape, lambda i: (0,) * a.ndim)
    ospec = pl.BlockSpec((PEER_HEADS, nk, RT_T), lambda i: (0, 0, i))
    return pl.pallas_call(
        _route_body,
        grid=(t // RT_T,),
        in_specs=[pl.BlockSpec((RT_T, D_MODEL), lambda i: (i, 0)),
                  full(m_fold), full(sel1), full(sel2), full(sel1t)],
        out_specs=[ospec, ospec, ospec, ospec],
        out_shape=[jax.ShapeDtypeStruct((PEER_HEADS, nk, t), BF16),
                   jax.ShapeDtypeStruct((PEER_HEADS, nk, t), BF16),
                   jax.ShapeDtypeStruct((PEER_HEADS, nk, t), F32),
                   jax.ShapeDtypeStruct((PEER_HEADS, nk, t), F32)],
        scratch_shapes=[pltpu.VMEM((2 * nk * PEER_HEADS, RT_T), F32)],
        compiler_params=_cparams(("parallel",)),
        name="route",
    )(h2b, m_fold, sel1, sel2, sel1t)


PE_T = 512
PE_E = 1024
SQRT_HALF = float(np.sqrt(0.5))


def _peer_body(h2_ref, u_ref, vt_ref, r2_ref, e2_ref, c1_ref, e1_ref, x1_ref, gtf_ref, gpost_ref,
               o_ref, acc_scr, gt_scr):
    e = pl.program_id(1)
    ne = pl.num_programs(1)

    @pl.when(e == 0)
    def _():
        acc_scr[...] = jnp.zeros(acc_scr.shape, F32)

    h2 = h2_ref[...]
    nk = PEER_N_KEYS
    for ii in range(PE_E // nk):
        i1 = e * (PE_E // nk) + ii
        a = lax.dot_general(u_ref[ii * nk:(ii + 1) * nk, :], h2, (((1,), (1,)), ((), ())),
                            preferred_element_type=F32)
        act = 0.5 * a * (1.0 + lax.erf(a * SQRT_HALF))
        p = jnp.zeros((nk // 16, 16, PE_T), BF16)
        for h in range(PEER_HEADS):
            c1t = jnp.broadcast_to(c1_ref[h, pl.ds(i1, 1), :], (16, PE_T)).astype(BF16)
            e1t = jnp.broadcast_to(e1_ref[h, pl.ds(i1, 1), :], (16, PE_T)).astype(BF16)
            r2 = r2_ref[h].reshape(nk // 16, 16, PE_T)
            e2 = e2_ref[h].reshape(nk // 16, 16, PE_T)
            p = p + jnp.where(r2 < c1t[None], e2 * e1t[None], jnp.zeros((), BF16))
        gt_scr[ii * nk:(ii + 1) * nk, :] = act.astype(BF16) * p.reshape(nk, PE_T)
    acc_scr[...] += jnp.dot(vt_ref[...], gt_scr[...], preferred_element_type=F32)

    @pl.when(e == ne - 1)
    def _():
        f = acc_scr[...].T
        o_ref[...] = x1_ref[...] + gtf_ref[...] * _rms(f, gpost_ref[...])


def _peer(h2b, u_b, vt_b, r2, e2, c1, e1, x1, mod4, gpost, row_of_tile):
    t = h2b.shape[0]
    nk = PEER_N_KEYS
    rspec = pl.BlockSpec((PEER_HEADS, nk, PE_T), lambda i, e: (0, 0, i))
    return pl.pallas_call(
        _peer_body,
        grid=(t // PE_T, PEER_EXPERTS // PE_E),
        in_specs=[pl.BlockSpec((PE_T, D_MODEL), lambda i, e: (i, 0)),
                  pl.BlockSpec((PE_E, D_MODEL), lambda i, e: (e, 0)),
                  pl.BlockSpec((D_MODEL, PE_E), lambda i, e: (0, e)),
                  rspec, rspec, rspec, rspec,
                  pl.BlockSpec((PE_T, D_MODEL), lambda i, e: (i, 0)),
                  pl.BlockSpec((None, None, 1, D_MODEL), lambda i, e: (row_of_tile(i), 5, 0, 0)),
                  pl.BlockSpec((1, D_MODEL), lambda i, e: (0, 0))],
        out_specs=pl.BlockSpec((PE_T, D_MODEL), lambda i, e: (i, 0)),
        out_shape=jax.ShapeDtypeStruct((t, D_MODEL), F32),
        scratch_shapes=[pltpu.VMEM((D_MODEL, PE_T), F32), pltpu.VMEM((PE_E, PE_T), BF16)],
        compiler_params=_cparams(("parallel", "arbitrary")),
        name="peer",
    )(h2b, u_b, vt_b, r2, e2, c1, e1, x1, mod4, gpost)


def _row_of_tile(tile_tokens, n_prompt_tokens, sample_len):
    n_prompt_tiles = n_prompt_tokens // tile_tokens
    tiles_per_seq = sample_len // tile_tokens

    def f(i):
        return jnp.where(i < n_prompt_tiles, 0, 1 + (i - n_prompt_tiles) // tiles_per_seq)
    return f


def kernel(x_prompt, x_sample, c, state_ssd, state_mlstm_c, state_mlstm_n, state_mlstm_m, c_ctx, w_mod, b_mod, g_mix_pre, g_mix_post, g_ffn_pre, g_ffn_post, w_in, conv_w, conv_b, ssd_a_log, ssd_dt_bias, ssd_d, ssd_norm_g, w_ssd_out, mlstm_i_bias, mlstm_f_bias, mlstm_norm_g, w_mlstm_out, w_out, peer_w_q, peer_keys1, peer_keys2, peer_u, peer_v):
    assert w_mod.shape[0] == 1, "single trunk layer"
    bp, lp_, d = x_prompt.shape
    bs, ls, _ = x_sample.shape
    assert d == D_MODEL and lp_ == SEQ and ls % IP_TM == 0 and (bp * lp_) % IP_TM == 0
    n_p, n_s = bp * lp_, bs * ls
    xp = x_prompt.reshape(n_p, d)
    xs = x_sample.reshape(n_s, d)

    cvec = jnp.zeros((8, d), F32).at[0].set(c_ctx).at[1:1 + bs].set(c)
    mod = _adaln(cvec, w_mod[0], b_mod[0])
    mod4 = mod.reshape(8, N_MOD, 1, d)

    wi = w_in[0]
    o_z, o_xbc, o_dt = 0, SSD_WIDTH, SSD_WIDTH + 2 * D_MODEL
    o_q = o_dt + 2 * SSD_HEADS
    o_gate = o_q + 4 * D_MODEL
    o_merge = o_gate + 4 * MLSTM_HEADS
    w_big = jnp.concatenate([wi[:, o_z:o_dt], wi[:, o_q:o_gate], wi[:, o_merge:]], axis=1).astype(BF16)
    n_small = 2 * SSD_HEADS + 4 * MLSTM_HEADS
    w_small = jnp.concatenate([wi[:, o_dt:o_q], wi[:, o_gate:o_merge],
                               jnp.zeros((d, SMALL_COLS - n_small), F32)], axis=1)
    gate_bias = jnp.stack([mlstm_i_bias[0], mlstm_f_bias[0]], axis=1).reshape(-1)
    b_small = jnp.concatenate([ssd_dt_bias[0].reshape(-1), gate_bias,
                               jnp.zeros((SMALL_COLS - n_small,), F32)]).reshape(1, SMALL_COLS)
    cw = jnp.zeros((8, BIG_COLS), F32)
    cw = cw.at[0:3, o_xbc:o_dt].set(conv_w[0]).at[3, o_xbc:o_dt].set(conv_b[0])

    row_ip = _row_of_tile(IP_TM, n_p, ls)
    p_big, small = _inproj(xp, xs, mod4, g_mix_pre, w_big, w_small, b_small, cw, row_ip, n_p // IP_TM)

    lo = np.tril(np.ones((CHUNK, CHUNK), np.float32))
    tri = jnp.asarray(np.stack([lo, lo.T]))
    a_log = jnp.zeros((2, 1, SMALL_COLS), F32).at[:, 0, :SSD_HEADS].set(ssd_a_log[0])

    ncp, ncs = lp_ // CHUNK, ls // CHUNK
    y_p, st_ssd = _ssd(p_big, small, tri, a_log, None, 0, bp, ncp, True)
    (y_s,) = _ssd(p_big, small, tri, a_log, state_ssd[:, 0].reshape(bs, 2, SSD_WIDTH, SSD_STATE),
                  n_p // CHUNK, bs, ncs, False)

    gates = small[:, 2 * SSD_HEADS:n_small].reshape(n_p + n_s, 2, 2 * MLSTM_HEADS)
    gcol = gates.transpose(1, 0, 2)
    grow = gcol.reshape(2, (n_p + n_s) // CHUNK, CHUNK, 2 * MLSTM_HEADS).transpose(0, 1, 3, 2)
    hm_p, st_c, st_n, st_m = _mlstm(p_big, gcol, grow, tri, None, 0, bp, ncp, True)
    init = (state_mlstm_c[:, 0],
            state_mlstm_n[:, 0].reshape(bs, 2, MLSTM_HEADS, 1, MLSTM_HEAD_DIM),
            jnp.broadcast_to(state_mlstm_m[:, 0][..., None, None], (bs, 2, MLSTM_HEADS, 1, 128)))
    (hm_s,) = _mlstm(p_big, gcol, grow, tri, init, n_p // CHUNK, bs, ncs, False)

    dskip = jnp.repeat(ssd_d[0, 0] + ssd_d[0, 1], SSD_HEAD_DIM).reshape(1, SSD_WIDTH)
    row_mg = _row_of_tile(MG_TM, n_p, ls)
    x1, h2b = _merge(y_p, y_s, hm_p, hm_s, xp, xs, p_big, mod4,
                     w_ssd_out[0].astype(BF16), w_mlstm_out[0].astype(BF16), w_out[0].astype(BF16),
                     ssd_norm_g, mlstm_norm_g, g_mix_post, g_ffn_pre, dskip, row_mg)

    sel1 = np.zeros((N_CAND_PAD, PEER_TOPK), np.float32)
    sel2 = np.zeros((N_CAND_PAD, PEER_TOPK), np.float32)
    for r, (j, l) in enumerate(_CAND):
        sel1[r, j] = 1.0
        sel2[r, l] = 1.0
    m_fold = _fold_keys(jnp.stack([peer_keys1[0], peer_keys2[0]]), peer_w_q[0].T)
    r2, e2, c1, e1 = _route(h2b, m_fold, jnp.asarray(sel1), jnp.asarray(sel2), jnp.asarray(sel1.T.copy()))

    row_pe = _row_of_tile(PE_T, n_p, ls)
    x2 = _peer(h2b, peer_u[0].astype(BF16), peer_v[0].T.astype(BF16), r2, e2, c1, e1, x1, mod4,
               g_ffn_post, row_pe)

    y_prompt = x2[:n_p].reshape(bp, lp_, d)
    y_sample = x2[n_p:].reshape(bs, ls, d)
    new_ssd = st_ssd.reshape(bp, 1, 2, SSD_HEADS, SSD_HEAD_DIM, SSD_STATE)
    new_c = st_c.reshape(bp, 1, 2, MLSTM_HEADS, MLSTM_HEAD_DIM, MLSTM_HEAD_DIM)
    new_n = st_n.reshape(bp, 1, 2, MLSTM_HEADS, MLSTM_HEAD_DIM)
    new_m = st_m[..., 0, 0].reshape(bp, 1, 2, MLSTM_HEADS)
    return (y_prompt, y_sample, new_ssd, new_c, new_n, new_m)
```

```python
import functools

import numpy as np
import jax
import jax.numpy as jnp
from jax import lax
from jax.experimental import pallas as pl
from jax.experimental.pallas import tpu as pltpu

F32 = jnp.float32
BF16 = jnp.bfloat16
HIGHEST = lax.Precision.HIGHEST

D_MODEL = 1024
EPS = 1e-6
N_MOD = 6
CHUNK = 128
SEQ = 256
GRID_W = 64
SSD_HEADS = 16
SSD_HEAD_DIM = 64
SSD_STATE = 128
SSD_GROUPS = 4
SSD_WIDTH = SSD_HEADS * SSD_HEAD_DIM
MLSTM_HEADS = 4
MLSTM_HEAD_DIM = 256
PEER_HEADS = 8
PEER_N_KEYS = 128
PEER_TOPK = 16
PEER_EXPERTS = PEER_N_KEYS * PEER_N_KEYS
SMALL_COLS = 128
BIG_COLS = 9 * D_MODEL
NEG = -1e30

VMEM_LIMIT = 56 * 1024 * 1024

_CAND = [(j, l) for j in range(PEER_TOPK) for l in range(PEER_TOPK) if (j + 1) * (l + 1) <= PEER_TOPK]
N_CAND = len(_CAND)
N_CAND_PAD = 56


def _cparams(sem):
    return pltpu.CompilerParams(dimension_semantics=sem, vmem_limit_bytes=VMEM_LIMIT)


def _sigmoid(x):
    return 1.0 / (1.0 + jnp.exp(-x))


def _silu(x):
    return x * _sigmoid(x)


def _rms(x, g):
    return x * lax.rsqrt(jnp.mean(x * x, axis=-1, keepdims=True) + EPS) * g


def _adaln_body(c_ref, w_ref, b_ref, o_ref):
    cv = c_ref[...]
    o_ref[...] = jnp.dot(_silu(cv), w_ref[...], precision=HIGHEST, preferred_element_type=F32) + b_ref[...]


def _adaln(cvec, w_mod, b_mod):
    tn = 512
    n = w_mod.shape[1]
    return pl.pallas_call(
        _adaln_body,
        grid=(n // tn,),
        in_specs=[pl.BlockSpec((8, D_MODEL), lambda j: (0, 0)),
                  pl.BlockSpec((D_MODEL, tn), lambda j: (0, j)),
                  pl.BlockSpec((1, tn), lambda j: (0, j))],
        out_specs=pl.BlockSpec((8, tn), lambda j: (0, j)),
        out_shape=jax.ShapeDtypeStruct((8, n), F32),
        compiler_params=_cparams(("arbitrary",)),
        name="adaln",
    )(cvec, w_mod, b_mod.reshape(1, n))


IP_TM = 1024
IP_TN = 512
IP_SUB = 256


def _inproj_body(n_prompt_tiles, xp_ref, xs_ref, sh_ref, sc_ref, g_ref, w_ref, ws_ref, bs_ref, cw_ref,
                 p_ref, s_ref, h_scr):
    i = pl.program_id(0)
    j = pl.program_id(1)

    @pl.when(j == 0)
    def _():
        x = jnp.where(i < n_prompt_tiles, xp_ref[...], xs_ref[...])
        h = _rms(x, g_ref[...]) * (1.0 + sc_ref[...]) + sh_ref[...]
        h_scr[...] = h.astype(BF16)
        sm = jnp.dot(h, ws_ref[...], precision=HIGHEST, preferred_element_type=F32) + bs_ref[...]
        col = lax.broadcasted_iota(jnp.int32, sm.shape, 1)
        is_f = (col >= 32) & (col < 48) & ((((col - 32) // MLSTM_HEADS) % 2) == 1)
        lp = jnp.log1p(jnp.exp(-jnp.abs(sm)))
        softplus = jnp.maximum(sm, 0.0) + lp
        logsig = jnp.minimum(sm, 0.0) - lp
        s_ref[...] = jnp.where(col < 32, softplus, jnp.where(is_f, logsig, sm))

    def conv_silu(acc):
        seg_mask = jnp.where(i < n_prompt_tiles, SEQ - 1, GRID_W - 1)
        pos = lax.broadcasted_iota(jnp.int32, acc.shape, 0) & seg_mask
        prev = jnp.where(pos == 0, 0.0, pltpu.roll(acc, 1, 0))
        nxt = jnp.where(pos == seg_mask, 0.0, pltpu.roll(acc, IP_SUB - 1, 0))
        cw = cw_ref[...]
        return _silu(cw[3:4] + cw[0:1] * prev + cw[1:2] * acc + cw[2:3] * nxt)

    def project(epilogue):
        for r in range(IP_TM // IP_SUB):
            rows = slice(r * IP_SUB, (r + 1) * IP_SUB)
            acc = jnp.dot(h_scr[rows, :], w_ref[...], preferred_element_type=F32)
            p_ref[rows, :] = epilogue(acc).astype(BF16)

    pl.when(j < 2)(lambda: project(_silu))
    pl.when((j >= 2) & (j < 6))(lambda: project(conv_silu))
    pl.when((j >= 6) & (j < 12))(lambda: project(lambda acc: acc))
    pl.when(j >= 12)(lambda: project(_sigmoid))


def _split_specs(tm, npt):
    return [pl.BlockSpec((tm, D_MODEL), lambda i, *_: (jnp.minimum(i, npt - 1), 0)),
            pl.BlockSpec((tm, D_MODEL), lambda i, *_: (jnp.maximum(i - npt, 0), 0))]


def _inproj(xp, xs, mod4, g, w_big, w_small, b_small, cw, row_of_tile, n_prompt_tiles):
    t = xp.shape[0] + xs.shape[0]
    grid = (t // IP_TM, BIG_COLS // IP_TN)
    return pl.pallas_call(
        functools.partial(_inproj_body, n_prompt_tiles),
        grid=grid,
        in_specs=_split_specs(IP_TM, n_prompt_tiles) + [
                  pl.BlockSpec((None, None, 1, D_MODEL), lambda i, j: (row_of_tile(i), 0, 0, 0)),
                  pl.BlockSpec((None, None, 1, D_MODEL), lambda i, j: (row_of_tile(i), 1, 0, 0)),
                  pl.BlockSpec((1, D_MODEL), lambda i, j: (0, 0)),
                  pl.BlockSpec((D_MODEL, IP_TN), lambda i, j: (0, j)),
                  pl.BlockSpec((D_MODEL, SMALL_COLS), lambda i, j: (0, 0)),
                  pl.BlockSpec((1, SMALL_COLS), lambda i, j: (0, 0)),
                  pl.BlockSpec((8, IP_TN), lambda i, j: (0, j))],
        out_specs=[pl.BlockSpec((IP_TM, IP_TN), lambda i, j: (i, j)),
                   pl.BlockSpec((IP_TM, SMALL_COLS), lambda i, j: (i, 0))],
        out_shape=[jax.ShapeDtypeStruct((t, BIG_COLS), BF16),
                   jax.ShapeDtypeStruct((t, SMALL_COLS), F32)],
        scratch_shapes=[pltpu.VMEM((IP_TM, D_MODEL), BF16)],
        compiler_params=_cparams(("parallel", "arbitrary")),
        name="inproj",
    )(xp, xs, mod4, mod4, g, w_big, w_small, b_small, cw)


def _ssd_body(has_h0, emit_state, *refs):
    refs = list(refs)
    xs_ref, bc_ref, sm_ref, tri_ref, alog_ref, exp_ref = refs[:6]
    k = 6
    h0_ref = None
    if has_h0:
        h0_ref = refs[k]
        k += 1
    y_ref = refs[k]
    k += 1
    hfin_ref = None
    if emit_state:
        hfin_ref = refs[k]
        k += 1
    hs_scr = refs[k]

    d = pl.program_id(0)
    c = pl.program_id(2)
    nc = pl.num_programs(2)

    @pl.when(c == 0)
    def _():
        if has_h0:
            hs_scr[...] = h0_ref[...].T
        else:
            hs_scr[...] = jnp.zeros(hs_scr.shape, F32)

    tri = tri_ref[...]
    keep = tri > 0.0
    sm = sm_ref[...]
    lane = lax.broadcasted_iota(jnp.int32, sm.shape, 1)
    dt = jnp.where(d == 0, sm, pltpu.roll(sm, SMALL_COLS - SSD_HEADS, 1))
    a_row = -jnp.exp(alog_ref[...])
    dta = jnp.where(lane < SSD_HEADS, dt * a_row, 0.0)
    cum = jnp.dot(tri, dta, precision=HIGHEST, preferred_element_type=F32)
    cum_t = cum.T
    tot = jnp.where(d == 0, cum[CHUNK - 1:CHUNK, :], cum[0:1, :])
    per_head = jnp.concatenate([jnp.where(lane < SSD_HEADS, dt, 0.0), jnp.exp(cum), jnp.exp(tot - cum),
                                jnp.broadcast_to(jnp.exp(tot), (8, SMALL_COLS))], axis=0)
    hi = per_head.astype(BF16)
    lo = (per_head - hi.astype(F32)).astype(BF16)
    expand = exp_ref[...]
    wide = (jnp.dot(hi, expand, preferred_element_type=F32)
            + jnp.dot(lo, expand, preferred_element_type=F32))
    dt_w = wide[0:CHUNK]
    decay_in_w = wide[CHUNK:2 * CHUNK]
    decay_out_w = wide[2 * CHUNK:3 * CHUNK]
    decay_tot_w = wide[3 * CHUNK:3 * CHUNK + 1]
    xdt = xs_ref[...].astype(F32) * dt_w
    xdt_b = xdt.astype(BF16)
    xdec_b = (xdt * decay_out_w).astype(BF16)
    low_half = lax.broadcasted_iota(jnp.int32, (CHUNK, 2 * SSD_HEAD_DIM), 1) < SSD_HEAD_DIM
    zero_b = jnp.zeros((CHUNK, 2 * SSD_HEAD_DIM), BF16)

    bc = bc_ref[...]
    hpg = SSD_HEADS // SSD_GROUPS
    gw = hpg * SSD_HEAD_DIM
    for g in range(SSD_GROUPS):
        b_g = bc[:, g * SSD_STATE:(g + 1) * SSD_STATE]
        c_g = bc[:, (SSD_GROUPS + g) * SSD_STATE:(SSD_GROUPS + g + 1) * SSD_STATE]
        cb = lax.dot_general(c_g, b_g, (((1,), (1,)), ((), ())), preferred_element_type=F32)
        gs = slice(g * gw, (g + 1) * gw)
        hs_g = hs_scr[:, gs]
        y_off = jnp.dot(c_g, hs_g.astype(BF16), preferred_element_type=F32) * decay_in_w[:, gs]
        for pr in range(hpg // 2):
            h0 = g * hpg + 2 * pr
            ps = slice(h0 * SSD_HEAD_DIM, (h0 + 2) * SSD_HEAD_DIM)
            s_pair = []
            for h in (h0, h0 + 1):
                lmat = jnp.exp(jnp.where(keep, cum[:, h:h + 1] - cum_t[h:h + 1, :], NEG))
                s_pair.append((cb * lmat).astype(BF16))
            x_pair = xdt_b[:, ps]
            rhs = jnp.concatenate([jnp.where(low_half, x_pair, zero_b),
                                   jnp.where(low_half, zero_b, x_pair)], axis=0)
            y_ref[:, ps] = (jnp.dot(jnp.concatenate(s_pair, axis=1), rhs, preferred_element_type=F32)
                            + y_off[:, 2 * pr * SSD_HEAD_DIM:(2 * pr + 2) * SSD_HEAD_DIM])
        upd = lax.dot_general(b_g, xdec_b[:, gs], (((0,), (0,)), ((), ())), preferred_element_type=F32)
        hs_scr[:, gs] = decay_tot_w[:, gs] * hs_g + upd

    if emit_state:
        @pl.when(c == nc - 1)
        def _():
            hfin_ref[...] = hs_scr[...].T


def _chunk_index(cbase, nc):
    def f(d, b, c):
        return cbase + b * nc + c + d * (nc - 1 - 2 * c)
    return f


def _ssd(p_big, small, tri, a_log, h0, cbase, nseq, nc, emit_state):
    ci = _chunk_index(cbase, nc)
    co = _chunk_index(0, nc)
    has_h0 = h0 is not None
    in_specs = [pl.BlockSpec((CHUNK, D_MODEL), lambda d, b, c: (ci(d, b, c), 1)),
                pl.BlockSpec((CHUNK, D_MODEL), lambda d, b, c: (ci(d, b, c), 2)),
                pl.BlockSpec((CHUNK, SMALL_COLS), lambda d, b, c: (ci(d, b, c), 0)),
                pl.BlockSpec((None, CHUNK, CHUNK), lambda d, b, c: (d, 0, 0)),
                pl.BlockSpec((None, 1, SMALL_COLS), lambda d, b, c: (d, 0, 0)),
                pl.BlockSpec((SMALL_COLS, SSD_WIDTH), lambda d, b, c: (0, 0))]
    head_of_lane = np.arange(SSD_WIDTH) // SSD_HEAD_DIM
    expand = jnp.asarray(np.arange(SMALL_COLS)[:, None] == head_of_lane[None, :], BF16)
    args = [p_big, p_big, small, tri, a_log, expand]
    if has_h0:
        in_specs.append(pl.BlockSpec((None, None, SSD_WIDTH, SSD_STATE), lambda d, b, c: (b, d, 0, 0)))
        args.append(h0)
    out_specs = [pl.BlockSpec((None, CHUNK, SSD_WIDTH), lambda d, b, c: (d, co(d, b, c), 0))]
    out_shape = [jax.ShapeDtypeStruct((2, nseq * nc * CHUNK, SSD_WIDTH), F32)]
    if emit_state:
        out_specs.append(pl.BlockSpec((None, None, SSD_WIDTH, SSD_STATE), lambda d, b, c: (b, d, 0, 0)))
        out_shape.append(jax.ShapeDtypeStruct((nseq, 2, SSD_WIDTH, SSD_STATE), F32))
    return pl.pallas_call(
        functools.partial(_ssd_body, has_h0, emit_state),
        grid=(2, nseq, nc),
        in_specs=in_specs,
        out_specs=out_specs,
        out_shape=out_shape,
        scratch_shapes=[pltpu.VMEM((SSD_STATE, SSD_WIDTH), F32)],
        compiler_params=_cparams(("arbitrary", "arbitrary", "arbitrary")),
        name="ssd_state" if emit_state else "ssd_seeded",
    )(*args)


def _mlstm_body(has_init, emit_state, *refs):
    refs = list(refs)
    q_ref, k_ref, v_ref, gc_ref, gr_ref, tri_ref, trit_ref = refs[:7]
    k_i = 7
    c0_ref = n0_ref = m0_ref = None
    if has_init:
        c0_ref, n0_ref, m0_ref = refs[k_i:k_i + 3]
        k_i += 3
    hm_ref = refs[k_i]
    k_i += 1
    cfin_ref = nfin_ref = mfin_ref = None
    if emit_state:
        cfin_ref, nfin_ref, mfin_ref = refs[k_i:k_i + 3]
        k_i += 3
    c_scr, n_scr, m_scr = refs[k_i:k_i + 3]

    c = pl.program_id(2)
    nc = pl.num_programs(2)

    @pl.when(c == 0)
    def _():
        if has_init:
            c_scr[...] = c0_ref[...]
            n_scr[...] = n0_ref[...]
            m_scr[...] = m0_ref[...]
        else:
            c_scr[...] = jnp.zeros(c_scr.shape, F32)
            n_scr[...] = jnp.zeros(n_scr.shape, F32)
            m_scr[...] = jnp.zeros(m_scr.shape, F32)

    tri = tri_ref[...]
    tri_t = trit_ref[...]
    keep = tri > 0.0
    gc = gc_ref[...]
    gr = gr_ref[...]
    dh = MLSTM_HEAD_DIM
    nh = MLSTM_HEADS
    heads = range(nh)
    sls = [slice(hh * dh, (hh + 1) * dh) for hh in heads]
    i_col = [gc[:, hh:hh + 1] for hh in heads]
    f_col = [gc[:, nh + hh:nh + hh + 1] for hh in heads]
    i_row = [gr[hh:hh + 1, :] for hh in heads]
    f_row = [gr[nh + hh:nh + hh + 1, :] for hh in heads]
    b_col = [jnp.sum(tri * f_row[hh], axis=1, keepdims=True) for hh in heads]
    b_row = [jnp.sum(tri_t * f_col[hh], axis=0, keepdims=True) for hh in heads]
    f_tot = [jnp.sum(f_row[hh], axis=1, keepdims=True) for hh in heads]
    m_prev = [m_scr[hh][:, 0:1] for hh in heads]
    q_h = [q_ref[:, sls[hh]] for hh in heads]
    k_h = [k_ref[:, sls[hh]] * (dh ** -0.5) for hh in heads]
    v_h = [v_ref[:, sls[hh]] for hh in heads]
    qk = [lax.dot_general(q_h[hh], k_h[hh], (((1,), (1,)), ((), ())), preferred_element_type=F32) for hh in heads]
    dlog = [jnp.where(keep, b_col[hh] - b_row[hh] + i_row[hh], NEG) for hh in heads]
    inter = [b_col[hh] + m_prev[hh] for hh in heads]
    m_t = [jnp.maximum(inter[hh], jnp.max(dlog[hh], axis=1, keepdims=True)) for hh in heads]
    s = [qk[hh] * jnp.exp(dlog[hh] - m_t[hh]) for hh in heads]
    w_int = [jnp.exp(inter[hh] - m_t[hh]) for hh in heads]
    c_prev = [c_scr[hh] for hh in heads]
    n_prev = [n_scr[hh] for hh in heads]
    num, den = [], []
    for hh in heads:
        q_f = q_h[hh].astype(F32)
        lhs = jnp.concatenate([s[hh].astype(BF16), (q_f * w_int[hh]).astype(BF16)], axis=1)
        rhs = jnp.concatenate([v_h[hh], c_prev[hh].astype(BF16)], axis=0)
        num.append(jnp.dot(lhs, rhs, preferred_element_type=F32))
        den.append(jnp.sum(s[hh], axis=1, keepdims=True)
                   + w_int[hh] * jnp.sum(q_f * n_prev[hh], axis=1, keepdims=True))
    g_col = [f_tot[hh] - b_col[hh] + i_col[hh] for hh in heads]
    m_loc = [jnp.max(g_col[hh], axis=0, keepdims=True) for hh in heads]
    kw = [k_h[hh].astype(F32) * jnp.exp(g_col[hh] - m_loc[hh]) for hh in heads]
    c_loc = [lax.dot_general(kw[hh].astype(BF16), v_h[hh], (((0,), (0,)), ((), ())), preferred_element_type=F32)
             for hh in heads]
    for hh in heads:
        hm_ref[:, sls[hh]] = num[hh] / jnp.maximum(jnp.abs(den[hh]), jnp.exp(-m_t[hh]))
    for hh in heads:
        n_loc = jnp.sum(kw[hh], axis=0, keepdims=True)
        m_new = jnp.maximum(f_tot[hh] + m_prev[hh], m_loc[hh])
        wa = jnp.exp(f_tot[hh] + m_prev[hh] - m_new)
        wb = jnp.exp(m_loc[hh] - m_new)
        c_scr[hh] = wa * c_prev[hh] + wb * c_loc[hh]
        n_scr[hh] = wa * n_prev[hh] + wb * n_loc
        m_scr[hh] = jnp.broadcast_to(m_new, (1, 128))

    if emit_state:
        @pl.when(c == nc - 1)
        def _():
            cfin_ref[...] = c_scr[...]
            nfin_ref[...] = n_scr[...]
            mfin_ref[...] = m_scr[...]


def _mlstm(p_big, gcol, grow, tri, init, cbase, nseq, nc, emit_state):
    ci = _chunk_index(cbase, nc)
    co = _chunk_index(0, nc)
    has_init = init is not None
    dh, nh = MLSTM_HEAD_DIM, MLSTM_HEADS
    in_specs = [pl.BlockSpec((CHUNK, D_MODEL), lambda d, b, c: (ci(d, b, c), 3)),
                pl.BlockSpec((CHUNK, D_MODEL), lambda d, b, c: (ci(d, b, c), 4)),
                pl.BlockSpec((CHUNK, D_MODEL), lambda d, b, c: (ci(d, b, c), 5)),
                pl.BlockSpec((None, CHUNK, 2 * nh), lambda d, b, c: (d, ci(d, b, c), 0)),
                pl.BlockSpec((None, None, 2 * nh, CHUNK), lambda d, b, c: (d, ci(d, b, c), 0, 0)),
                pl.BlockSpec((None, CHUNK, CHUNK), lambda d, b, c: (d, 0, 0)),
                pl.BlockSpec((None, CHUNK, CHUNK), lambda d, b, c: (1 - d, 0, 0))]
    args = [p_big, p_big, p_big, gcol, grow, tri, tri]
    if has_init:
        in_specs += [pl.BlockSpec((None, None, nh, dh, dh), lambda d, b, c: (b, d, 0, 0, 0)),
                     pl.BlockSpec((None, None, nh, 1, dh), lambda d, b, c: (b, d, 0, 0, 0)),
                     pl.BlockSpec((None, None, nh, 1, 128), lambda d, b, c: (b, d, 0, 0, 0))]
        args += list(init)
    out_specs = [pl.BlockSpec((None, CHUNK, D_MODEL), lambda d, b, c: (d, co(d, b, c), 0))]
    out_shape = [jax.ShapeDtypeStruct((2, nseq * nc * CHUNK, D_MODEL), F32)]
    if emit_state:
        out_specs += [pl.BlockSpec((None, None, nh, dh, dh), lambda d, b, c: (b, d, 0, 0, 0)),
                      pl.BlockSpec((None, None, nh, 1, dh), lambda d, b, c: (b, d, 0, 0, 0)),
                      pl.BlockSpec((None, None, nh, 1, 128), lambda d, b, c: (b, d, 0, 0, 0))]
        out_shape += [jax.ShapeDtypeStruct((nseq, 2, nh, dh, dh), F32),
                      jax.ShapeDtypeStruct((nseq, 2, nh, 1, dh), F32),
                      jax.ShapeDtypeStruct((nseq, 2, nh, 1, 128), F32)]
    return pl.pallas_call(
        functools.partial(_mlstm_body, has_init, emit_state),
        grid=(2, nseq, nc),
        in_specs=in_specs,
        out_specs=out_specs,
        out_shape=out_shape,
        scratch_shapes=[pltpu.VMEM((nh, dh, dh), F32), pltpu.VMEM((nh, 1, dh), F32), pltpu.VMEM((nh, 1, 128), F32)],
        compiler_params=_cparams(("arbitrary", "arbitrary", "arbitrary")),
        name="mlstm_state" if emit_state else "mlstm_seeded",
    )(*args)


MG_TM = 256


def _merge_body(n_prompt_tiles, yp_ref, ys_ref, hmp_ref, hms_ref, xp_ref, xsm_ref,
                z_ref, xs_ref, o_ref, g1_ref, g2_ref, gta_ref, shf_ref, scf_ref,
                wso_ref, wmo_ref, wout_ref, gssd_ref, gml_ref, gpost_ref, gpre_ref, dskip_ref,
                x1_ref, h2b_ref):
    is_prompt = pl.program_id(0) < n_prompt_tiles
    y = jnp.where(is_prompt, yp_ref[0] + yp_ref[1], ys_ref[0] + ys_ref[1])
    y = y + dskip_ref[...] * xs_ref[...].astype(F32)
    y = _rms(y * z_ref[...].astype(F32), gssd_ref[...])
    y_ssd = jnp.dot(y.astype(BF16), wso_ref[...], preferred_element_type=F32)
    hm = jnp.where(is_prompt, hmp_ref[0] + hmp_ref[1], hms_ref[0] + hms_ref[1])
    gml = gml_ref[...]
    parts = []
    for hh in range(MLSTM_HEADS):
        sl = slice(hh * MLSTM_HEAD_DIM, (hh + 1) * MLSTM_HEAD_DIM)
        parts.append(_rms(hm[:, sl], gml[:, sl]))
    hm = jnp.concatenate(parts, axis=1) * o_ref[...].astype(F32)
    y_ml = jnp.dot(hm.astype(BF16), wmo_ref[...], preferred_element_type=F32)
    mixed = g1_ref[...].astype(F32) * y_ssd + g2_ref[...].astype(F32) * y_ml
    mixed = jnp.dot(mixed.astype(BF16), wout_ref[...], preferred_element_type=F32)
    x = jnp.where(is_prompt, xp_ref[...], xsm_ref[...])
    x1 = x + gta_ref[...] * _rms(mixed, gpost_ref[...])
    x1_ref[...] = x1
    h2 = _rms(x1, gpre_ref[...]) * (1.0 + scf_ref[...]) + shf_ref[...]
    h2b_ref[...] = h2.astype(BF16)


def _merge(y_p, y_s, hm_p, hm_s, xp, xs, p_big, mod4, wso, wmo, wout, gssd, gml, gpost, gpre, dskip,
           row_of_tile):
    t = xp.shape[0] + xs.shape[0]
    tm = MG_TM
    npt = xp.shape[0] // tm
    tok = lambda col: pl.BlockSpec((tm, D_MODEL), lambda i: (i, col))
    both_p = pl.BlockSpec((2, tm, D_MODEL), lambda i: (0, jnp.minimum(i, npt - 1), 0))
    both_s = pl.BlockSpec((2, tm, D_MODEL), lambda i: (0, jnp.maximum(i - npt, 0), 0))
    modrow = lambda idx: pl.BlockSpec((None, None, 1, D_MODEL), lambda i: (row_of_tile(i), idx, 0, 0))
    vec = pl.BlockSpec((1, D_MODEL), lambda i: (0, 0))
    wspec = pl.BlockSpec((D_MODEL, D_MODEL), lambda i: (0, 0))
    return pl.pallas_call(
        functools.partial(_merge_body, npt),
        grid=(t // tm,),
        in_specs=[both_p, both_s, both_p, both_s] + _split_specs(tm, npt) + [
                  tok(0), tok(1), tok(6), tok(7), tok(8),
                  modrow(2), modrow(3), modrow(4),
                  wspec, wspec, wspec, vec, vec, vec, vec, vec],
        out_specs=[tok(0), tok(0)],
        out_shape=[jax.ShapeDtypeStruct((t, D_MODEL), F32),
                   jax.ShapeDtypeStruct((t, D_MODEL), BF16)],
        compiler_params=_cparams(("parallel",)),
        name="merge",
    )(y_p, y_s, hm_p, hm_s, xp, xs, p_big, p_big, p_big, p_big, p_big, mod4, mod4, mod4,
      wso, wmo, wout, gssd, gml, gpost, gpre, dskip)


RT_T = 256


def _top16(s, nrows):
    t = s.shape[1]
    rowid = lax.broadcasted_iota(jnp.int32, (nrows, t), 0).astype(F32)
    row16 = lax.broadcasted_iota(jnp.int32, (PEER_TOPK, t), 0)

    def body(j, carry):
        cur, rank, vals = carry
        m = jnp.max(cur, axis=0, keepdims=True)
        idx = jnp.min(jnp.where(cur == m, rowid, 1e9), axis=0, keepdims=True)
        sel = rowid == idx
        rank = jnp.where(sel, lax.convert_element_type(j, F32), rank)
        cur = jnp.where(sel, -jnp.inf, cur)
        vals = jnp.where(row16 == j, m, vals)
        return cur, rank, vals

    init = (s, jnp.full((nrows, t), 127.0, F32), jnp.zeros((PEER_TOPK, t), F32))
    _, rank, vals = lax.fori_loop(0, PEER_TOPK, body, init)
    return rank, vals


MARK = 2.0 ** 100


def _top16_pair_marked(sa, sb):
    lanes = sa.shape[1]
    row16 = lax.broadcasted_iota(jnp.int32, (PEER_TOPK, lanes), 0)

    def body(j, carry):
        ca, cb, va, vb = carry
        mk = -(lax.convert_element_type(j, F32) + 1.0) * MARK
        ma = jnp.max(ca, axis=0, keepdims=True)
        mb = jnp.max(cb, axis=0, keepdims=True)
        ca = jnp.where(ca == ma, mk, ca)
        cb = jnp.where(cb == mb, mk, cb)
        va = jnp.where(row16 == j, ma, va)
        vb = jnp.where(row16 == j, mb, vb)
        return ca, cb, va, vb

    zeros = jnp.zeros((PEER_TOPK, lanes), F32)
    return lax.fori_loop(0, PEER_TOPK, body, (sa, sb, zeros, zeros))


def _marked_rank(s, cur):
    took = cur <= -MARK
    rank = jnp.where(took, cur * (-1.0 / MARK) - 1.0, 127.0)
    n_took = jnp.sum(took.astype(F32), axis=0, keepdims=True)
    bad = (n_took != float(PEER_TOPK)) | (jnp.min(s, axis=0, keepdims=True) <= -0.25 * MARK)
    return rank, bad.astype(F32)


def _chosen_marked(cand):
    def body(j, cur):
        m = jnp.max(cur, axis=0, keepdims=True)
        return jnp.where(cur == m, -2.0 * MARK, cur)

    cur = lax.fori_loop(0, PEER_TOPK, body, cand)
    took = cur <= -MARK
    n_took = jnp.sum(took.astype(F32), axis=0, keepdims=True)
    return took, (n_took != float(PEER_TOPK)).astype(F32)


def _route_tail(marked, h, s1, s2, rank1, rank2, v1, v2,
                sel1_ref, sel2_ref, sel1t_ref, r2_ref, e2_ref, c1_ref, e1_ref):
    cand = (jnp.dot(sel1_ref[...], v1, precision=HIGHEST, preferred_element_type=F32)
            + jnp.dot(sel2_ref[...], v2, precision=HIGHEST, preferred_element_type=F32))
    crow = lax.broadcasted_iota(jnp.int32, cand.shape, 0)
    if marked:
        chosen, bad = _chosen_marked(jnp.where(crow < N_CAND, cand, -0.75 * MARK))
    else:
        cand = jnp.where(crow < N_CAND, cand, -jnp.inf)
        crank, _ = _top16(cand, N_CAND_PAD)
        chosen = crank < float(PEER_TOPK)
        bad = jnp.zeros((1, cand.shape[1]), F32)
    cmax = v1[0:1, :] + v2[0:1, :]
    z = jnp.sum(jnp.where(chosen, jnp.exp(cand - cmax), 0.0), axis=0, keepdims=True)
    cnt = jnp.dot(sel1t_ref[...], chosen.astype(F32), preferred_element_type=F32)
    c1 = jnp.zeros(s1.shape, F32)
    for j in range(PEER_TOPK):
        c1 = jnp.where(rank1 == float(j), cnt[j:j + 1, :], c1)
    r2_ref[h] = rank2.astype(BF16)
    e2_ref[h] = jnp.exp(s2 - v2[0:1, :]).astype(BF16)
    c1_ref[h] = c1
    e1_ref[h] = jnp.exp(s1 - v1[0:1, :]) / z
    return bad


def _route_body(h2_ref, m_ref, sel1_ref, sel2_ref, sel1t_ref, r2_ref, e2_ref, c1_ref, e1_ref, sc_scr):
    sc_scr[...] = lax.dot_general(m_ref[...], h2_ref[...], (((1,), (1,)), ((), ())), preferred_element_type=F32)
    nk = PEER_N_KEYS
    tail_refs = (sel1_ref, sel2_ref, sel1t_ref, r2_ref, e2_ref, c1_ref, e1_ref)
    for h in range(PEER_HEADS):
        s1 = sc_scr[h * 2 * nk:h * 2 * nk + nk, :]
        s2 = sc_scr[h * 2 * nk + nk:(h + 1) * 2 * nk, :]
        ranks1, ranks2, vals1, vals2 = [], [], [], []
        n_bad = jnp.zeros((1, 128), F32)
        for sub in range(RT_T // 128):
            ls = slice(sub * 128, (sub + 1) * 128)
            ca, cb, va, vb = _top16_pair_marked(s1[:, ls], s2[:, ls])
            ra, bad_a = _marked_rank(s1[:, ls], ca)
            rb, bad_b = _marked_rank(s2[:, ls], cb)
            n_bad = n_bad + bad_a + bad_b
            ranks1.append(ra)
            ranks2.append(rb)
            vals1.append(va)
            vals2.append(vb)
        cat = lambda parts: jnp.concatenate(parts, axis=1)
        bad_c = _route_tail(True, h, s1, s2, cat(ranks1), cat(ranks2), cat(vals1), cat(vals2), *tail_refs)

        @pl.when(jnp.sum(n_bad) + jnp.sum(bad_c) > 0.0)
        def _():
            rank1, v1 = _top16(s1, nk)
            rank2, v2 = _top16(s2, nk)
            _route_tail(False, h, s1, s2, rank1, rank2, v1, v2, *tail_refs)


def _fold_body(k_ref, wqt_ref, m_ref):
    m_ref[...] = jnp.dot(k_ref[...], wqt_ref[...], precision=HIGHEST, preferred_element_type=F32).astype(BF16)


def _fold_keys(keys, wqt):
    nk = PEER_N_KEYS
    nblk = wqt.shape[0] // nk
    return pl.pallas_call(
        _fold_body,
        grid=(nblk,),
        in_specs=[pl.BlockSpec((None, nk, nk), lambda r: (r % 2, 0, 0)),
                  pl.BlockSpec((nk, D_MODEL), lambda r: (r, 0))],
        out_specs=pl.BlockSpec((nk, D_MODEL), lambda r: (r, 0)),
        out_shape=jax.ShapeDtypeStruct(wqt.shape, BF16),
        compiler_params=_cparams(("parallel",)),
        name="fold_keys",
    )(keys, wqt)


def _route(h2b, m_fold, sel1, sel2, sel1t):
    t = h2b.shape[0]
    nk = PEER_N_KEYS
    full = lambda a: pl.BlockSpec(a.shape, lambda i: (0,) * a.ndim)
    ospec = pl.BlockSpec((PEER_HEADS, nk, RT_T), lambda i: (0, 0, i))
    return pl.pallas_call(
        _route_body,
        grid=(t // RT_T,),
        in_specs=[pl.BlockSpec((RT_T, D_MODEL), lambda i: (i, 0)),
                  full(m_fold), full(sel1), full(sel2), full(sel1t)],
        out_specs=[ospec, ospec, ospec, ospec],
        out_shape=[jax.ShapeDtypeStruct((PEER_HEADS, nk, t), BF16),
                   jax.ShapeDtypeStruct((PEER_HEADS, nk, t), BF16),
                   jax.ShapeDtypeStruct((PEER_HEADS, nk, t), F32),
                   jax.ShapeDtypeStruct((PEER_HEADS, nk, t), F32)],
        scratch_shapes=[pltpu.VMEM((2 * nk * PEER_HEADS, RT_T), F32)],
        compiler_params=_cparams(("parallel",)),
        name="route",
    )(h2b, m_fold, sel1, sel2, sel1t)


PE_T = 512
PE_E = 1024
SQRT_HALF = float(np.sqrt(0.5))


def _peer_body(h2_ref, u_ref, vt_ref, r2_ref, e2_ref, c1_ref, e1_ref, x1_ref, gtf_ref, gpost_ref,
               o_ref, acc_scr, gt_scr):
    e = pl.program_id(1)
    ne = pl.num_programs(1)

    @pl.when(e == 0)
    def _():
        acc_scr[...] = jnp.zeros(acc_scr.shape, F32)

    h2 = h2_ref[...]
    nk = PEER_N_KEYS
    for ii in range(PE_E // nk):
        i1 = e * (PE_E // nk) + ii
        a = lax.dot_general(u_ref[ii * nk:(ii + 1) * nk, :], h2, (((1,), (1,)), ((), ())),
                            preferred_element_type=F32)
        act = 0.5 * a * (1.0 + lax.erf(a * SQRT_HALF))
        p = jnp.zeros((nk // 16, 16, PE_T), BF16)
        for h in range(PEER_HEADS):
            c1t = jnp.broadcast_to(c1_ref[h, pl.ds(i1, 1), :], (16, PE_T)).astype(BF16)
            e1t = jnp.broadcast_to(e1_ref[h, pl.ds(i1, 1), :], (16, PE_T)).astype(BF16)
            r2 = r2_ref[h].reshape(nk // 16, 16, PE_T)
            e2 = e2_ref[h].reshape(nk // 16, 16, PE_T)
            p = p + jnp.where(r2 < c1t[None], e2 * e1t[None], jnp.zeros((), BF16))
        gt_scr[ii * nk:(ii + 1) * nk, :] = act.astype(BF16) * p.reshape(nk, PE_T)
    acc_scr[...] += jnp.dot(vt_ref[...], gt_scr[...], preferred_element_type=F32)

    @pl.when(e == ne - 1)
    def _():
        f = acc_scr[...].T
        o_ref[...] = x1_ref[...] + gtf_ref[...] * _rms(f, gpost_ref[...])


def _peer(h2b, u_b, vt_b, r2, e2, c1, e1, x1, mod4, gpost, row_of_tile):
    t = h2b.shape[0]
    nk = PEER_N_KEYS
    rspec = pl.BlockSpec((PEER_HEADS, nk, PE_T), lambda i, e: (0, 0, i))
    return pl.pallas_call(
        _peer_body,
        grid=(t // PE_T, PEER_EXPERTS // PE_E),
        in_specs=[pl.BlockSpec((PE_T, D_MODEL), lambda i, e: (i, 0)),
                  pl.BlockSpec((PE_E, D_MODEL), lambda i, e: (e, 0)),
                  pl.BlockSpec((D_MODEL, PE_E), lambda i, e: (0, e)),
                  rspec, rspec, rspec, rspec,
                  pl.BlockSpec((PE_T, D_MODEL), lambda i, e: (i, 0)),
                  pl.BlockSpec((None, None, 1, D_MODEL), lambda i, e: (row_of_tile(i), 5, 0, 0)),
                  pl.BlockSpec((1, D_MODEL), lambda i, e: (0, 0))],
        out_specs=pl.BlockSpec((PE_T, D_MODEL), lambda i, e: (i, 0)),
        out_shape=jax.ShapeDtypeStruct((t, D_MODEL), F32),
        scratch_shapes=[pltpu.VMEM((D_MODEL, PE_T), F32), pltpu.VMEM((PE_E, PE_T), BF16)],
        compiler_params=_cparams(("parallel", "arbitrary")),
        name="peer",
    )(h2b, u_b, vt_b, r2, e2, c1, e1, x1, mod4, gpost)


def _row_of_tile(tile_tokens, n_prompt_tokens, sample_len):
    n_prompt_tiles = n_prompt_tokens // tile_tokens
    tiles_per_seq = sample_len // tile_tokens

    def f(i):
        return jnp.where(i < n_prompt_tiles, 0, 1 + (i - n_prompt_tiles) // tiles_per_seq)
    return f


def kernel(x_prompt, x_sample, c, state_ssd, state_mlstm_c, state_mlstm_n, state_mlstm_m, c_ctx, w_mod, b_mod, g_mix_pre, g_mix_post, g_ffn_pre, g_ffn_post, w_in, conv_w, conv_b, ssd_a_log, ssd_dt_bias, ssd_d, ssd_norm_g, w_ssd_out, mlstm_i_bias, mlstm_f_bias, mlstm_norm_g, w_mlstm_out, w_out, peer_w_q, peer_keys1, peer_keys2, peer_u, peer_v):
    assert w_mod.shape[0] == 1, "single trunk layer"
    bp, lp_, d = x_prompt.shape
    bs, ls, _ = x_sample.shape
    assert d == D_MODEL and lp_ == SEQ and ls % IP_TM == 0 and (bp * lp_) % IP_TM == 0
    n_p, n_s = bp * lp_, bs * ls
    xp = x_prompt.reshape(n_p, d)
    xs = x_sample.reshape(n_s, d)

    cvec = jnp.zeros((8, d), F32).at[0].set(c_ctx).at[1:1 + bs].set(c)
    mod = _adaln(cvec, w_mod[0], b_mod[0])
    mod4 = mod.reshape(8, N_MOD, 1, d)

    wi = w_in[0]
    o_z, o_xbc, o_dt = 0, SSD_WIDTH, SSD_WIDTH + 2 * D_MODEL
    o_q = o_dt + 2 * SSD_HEADS
    o_gate = o_q + 4 * D_MODEL
    o_merge = o_gate + 4 * MLSTM_HEADS
    w_big = jnp.concatenate([wi[:, o_z:o_dt], wi[:, o_q:o_gate], wi[:, o_merge:]], axis=1).astype(BF16)
    n_small = 2 * SSD_HEADS + 4 * MLSTM_HEADS
    w_small = jnp.concatenate([wi[:, o_dt:o_q], wi[:, o_gate:o_merge],
                               jnp.zeros((d, SMALL_COLS - n_small), F32)], axis=1)
    gate_bias = jnp.stack([mlstm_i_bias[0], mlstm_f_bias[0]], axis=1).reshape(-1)
    b_small = jnp.concatenate([ssd_dt_bias[0].reshape(-1), gate_bias,
                               jnp.zeros((SMALL_COLS - n_small,), F32)]).reshape(1, SMALL_COLS)
    cw = jnp.zeros((8, BIG_COLS), F32)
    cw = cw.at[0:3, o_xbc:o_dt].set(conv_w[0]).at[3, o_xbc:o_dt].set(conv_b[0])

    row_ip = _row_of_tile(IP_TM, n_p, ls)
    p_big, small = _inproj(xp, xs, mod4, g_mix_pre, w_big, w_small, b_small, cw, row_ip, n_p // IP_TM)

    lo = np.tril(np.ones((CHUNK, CHUNK), np.float32))
    tri = jnp.asarray(np.stack([lo, lo.T]))
    a_log = jnp.zeros((2, 1, SMALL_COLS), F32).at[:, 0, :SSD_HEADS].set(ssd_a_log[0])

    ncp, ncs = lp_ // CHUNK, ls // CHUNK
    y_p, st_ssd = _ssd(p_big, small, tri, a_log, None, 0, bp, ncp, True)
    (y_s,) = _ssd(p_big, small, tri, a_log, state_ssd[:, 0].reshape(bs, 2, SSD_WIDTH, SSD_STATE),
                  n_p // CHUNK, bs, ncs, False)

    gates = small[:, 2 * SSD_HEADS:n_small].reshape(n_p + n_s, 2, 2 * MLSTM_HEADS)
    gcol = gates.transpose(1, 0, 2)
    grow = gcol.reshape(2, (n_p + n_s) // CHUNK, CHUNK, 2 * MLSTM_HEADS).transpose(0, 1, 3, 2)
    hm_p, st_c, st_n, st_m = _mlstm(p_big, gcol, grow, tri, None, 0, bp, ncp, True)
    init = (state_mlstm_c[:, 0],
            state_mlstm_n[:, 0].reshape(bs, 2, MLSTM_HEADS, 1, MLSTM_HEAD_DIM),
            jnp.broadcast_to(state_mlstm_m[:, 0][..., None, None], (bs, 2, MLSTM_HEADS, 1, 128)))
    (hm_s,) = _mlstm(p_big, gcol, grow, tri, init, n_p // CHUNK, bs, ncs, False)

    dskip = jnp.repeat(ssd_d[0, 0] + ssd_d[0, 1], SSD_HEAD_DIM).reshape(1, SSD_WIDTH)
    row_mg = _row_of_tile(MG_TM, n_p, ls)
    x1, h2b = _merge(y_p, y_s, hm_p, hm_s, xp, xs, p_big, mod4,
                     w_ssd_out[0].astype(BF16), w_mlstm_out[0].astype(BF16), w_out[0].astype(BF16),
                     ssd_norm_g, mlstm_norm_g, g_mix_post, g_ffn_pre, dskip, row_mg)

    sel1 = np.zeros((N_CAND_PAD, PEER_TOPK), np.float32)
    sel2 = np.zeros((N_CAND_PAD, PEER_TOPK), np.float32)
    for r, (j, l) in enumerate(_CAND):
        sel1[r, j] = 1.0
        sel2[r, l] = 1.0
    m_fold = _fold_keys(jnp.stack([peer_keys1[0], peer_keys2[0]]), peer_w_q[0].T)
    r2, e2, c1, e1 = _route(h2b, m_fold, jnp.asarray(sel1), jnp.asarray(sel2), jnp.asarray(sel1.T.copy()))

    row_pe = _row_of_tile(PE_T, n_p, ls)
    x2 = _peer(h2b, peer_u[0].astype(BF16), peer_v[0].T.astype(BF16), r2, e2, c1, e1, x1, mod4,
               g_ffn_post, row_pe)

    y_prompt = x2[:n_p].reshape(bp, lp_, d)
    y_sample = x2[n_p:].reshape(bs, ls, d)
    new_ssd = st_ssd.reshape(bp, 1, 2, SSD_HEADS, SSD_HEAD_DIM, SSD_STATE)
    new_c = st_c.reshape(bp, 1, 2, MLSTM_HEADS, MLSTM_HEAD_DIM, MLSTM_HEAD_DIM)
    new_n = st_n.reshape(bp, 1, 2, MLSTM_HEADS, MLSTM_HEAD_DIM)
    new_m = st_m[..., 0, 0].reshape(bp, 1, 2, MLSTM_HEADS)
    return (y_prompt, y_sample, new_ssd, new_c, new_n, new_m)
```

```python
import functools

import numpy as np
import jax
import jax.numpy as jnp
from jax import lax
from jax.experimental import pallas as pl
from jax.experimental.pallas import tpu as pltpu

F32 = jnp.float32
BF16 = jnp.bfloat16
HIGHEST = lax.Precision.HIGHEST

D_MODEL = 1024
EPS = 1e-6
N_MOD = 6
CHUNK = 128
SEQ = 256
GRID_W = 64
SSD_HEADS = 16
SSD_HEAD_DIM = 64
SSD_STATE = 128
SSD_GROUPS = 4
SSD_WIDTH = SSD_HEADS * SSD_HEAD_DIM
MLSTM_HEADS = 4
MLSTM_HEAD_DIM = 256
PEER_HEADS = 8
PEER_N_KEYS = 128
PEER_TOPK = 16
PEER_EXPERTS = PEER_N_KEYS * PEER_N_KEYS
SMALL_COLS = 128
BIG_COLS = 9 * D_MODEL
NEG = -1e30

VMEM_LIMIT = 56 * 1024 * 1024

_CAND = [(j, l) for j in range(PEER_TOPK) for l in range(PEER_TOPK) if (j + 1) * (l + 1) <= PEER_TOPK]
N_CAND = len(_CAND)
N_CAND_PAD = 56


def _cparams(sem):
    return pltpu.CompilerParams(dimension_semantics=sem, vmem_limit_bytes=VMEM_LIMIT)


def _sigmoid(x):
    return 1.0 / (1.0 + jnp.exp(-x))


def _silu(x):
    return x * _sigmoid(x)


def _rms(x, g):
    return x * lax.rsqrt(jnp.mean(x * x, axis=-1, keepdims=True) + EPS) * g


def _adaln_body(c_ref, w_ref, b_ref, o_ref):
    cv = c_ref[...]
    o_ref[...] = jnp.dot(_silu(cv), w_ref[...], precision=HIGHEST, preferred_element_type=F32) + b_ref[...]


def _adaln(cvec, w_mod, b_mod):
    tn = 512
    n = w_mod.shape[1]
    return pl.pallas_call(
        _adaln_body,
        grid=(n // tn,),
        in_specs=[pl.BlockSpec((8, D_MODEL), lambda j: (0, 0)),
                  pl.BlockSpec((D_MODEL, tn), lambda j: (0, j)),
                  pl.BlockSpec((1, tn), lambda j: (0, j))],
        out_specs=pl.BlockSpec((8, tn), lambda j: (0, j)),
        out_shape=jax.ShapeDtypeStruct((8, n), F32),
        compiler_params=_cparams(("arbitrary",)),
        name="adaln",
    )(cvec, w_mod, b_mod.reshape(1, n))


IP_TM = 1024
IP_TN = 512
IP_SUB = 256


def _inproj_body(n_prompt_tiles, xp_ref, xs_ref, sh_ref, sc_ref, g_ref, w_ref, ws_ref, bs_ref, cw_ref,
                 p_ref, s_ref, h_scr):
    i = pl.program_id(0)
    j = pl.program_id(1)

    @pl.when(j == 0)
    def _():
        x = jnp.where(i < n_prompt_tiles, xp_ref[...], xs_ref[...])
        h = _rms(x, g_ref[...]) * (1.0 + sc_ref[...]) + sh_ref[...]
        h_scr[...] = h.astype(BF16)
        sm = jnp.dot(h, ws_ref[...], precision=HIGHEST, preferred_element_type=F32) + bs_ref[...]
        col = lax.broadcasted_iota(jnp.int32, sm.shape, 1)
        is_f = (col >= 32) & (col < 48) & ((((col - 32) // MLSTM_HEADS) % 2) == 1)
        lp = jnp.log1p(jnp.exp(-jnp.abs(sm)))
        softplus = jnp.maximum(sm, 0.0) + lp
        logsig = jnp.minimum(sm, 0.0) - lp
        s_ref[...] = jnp.where(col < 32, softplus, jnp.where(is_f, logsig, sm))

    def conv_silu(acc):
        seg_mask = jnp.where(i < n_prompt_tiles, SEQ - 1, GRID_W - 1)
        pos = lax.broadcasted_iota(jnp.int32, acc.shape, 0) & seg_mask
        prev = jnp.where(pos == 0, 0.0, pltpu.roll(acc, 1, 0))
        nxt = jnp.where(pos == seg_mask, 0.0, pltpu.roll(acc, IP_SUB - 1, 0))
        cw = cw_ref[...]
        return _silu(cw[3:4] + cw[0:1] * prev + cw[1:2] * acc + cw[2:3] * nxt)

    def project(epilogue):
        for r in range(IP_TM // IP_SUB):
            rows = slice(r * IP_SUB, (r + 1) * IP_SUB)
            acc = jnp.dot(h_scr[rows, :], w_ref[...], preferred_element_type=F32)
            p_ref[rows, :] = epilogue(acc).astype(BF16)

    pl.when(j < 2)(lambda: project(_silu))
    pl.when((j >= 2) & (j < 6))(lambda: project(conv_silu))
    pl.when((j >= 6) & (j < 12))(lambda: project(lambda acc: acc))
    pl.when(j >= 12)(lambda: project(_sigmoid))


def _split_specs(tm, npt):
    return [pl.BlockSpec((tm, D_MODEL), lambda i, *_: (jnp.minimum(i, npt - 1), 0)),
            pl.BlockSpec((tm, D_MODEL), lambda i, *_: (jnp.maximum(i - npt, 0), 0))]


def _inproj(xp, xs, mod4, g, w_big, w_small, b_small, cw, row_of_tile, n_prompt_tiles):
    t = xp.shape[0] + xs.shape[0]
    grid = (t // IP_TM, BIG_COLS // IP_TN)
    return pl.pallas_call(
        functools.partial(_inproj_body, n_prompt_tiles),
        grid=grid,
        in_specs=_split_specs(IP_TM, n_prompt_tiles) + [
                  pl.BlockSpec((None, None, 1, D_MODEL), lambda i, j: (row_of_tile(i), 0, 0, 0)),
                  pl.BlockSpec((None, None, 1, D_MODEL), lambda i, j: (row_of_tile(i), 1, 0, 0)),
                  pl.BlockSpec((1, D_MODEL), lambda i, j: (0, 0)),
                  pl.BlockSpec((D_MODEL, IP_TN), lambda i, j: (0, j)),
                  pl.BlockSpec((D_MODEL, SMALL_COLS), lambda i, j: (0, 0)),
                  pl.BlockSpec((1, SMALL_COLS), lambda i, j: (0, 0)),
                  pl.BlockSpec((8, IP_TN), lambda i, j: (0, j))],
        out_specs=[pl.BlockSpec((IP_TM, IP_TN), lambda i, j: (i, j)),
                   pl.BlockSpec((IP_TM, SMALL_COLS), lambda i, j: (i, 0))],
        out_shape=[jax.ShapeDtypeStruct((t, BIG_COLS), BF16),
                   jax.ShapeDtypeStruct((t, SMALL_COLS), F32)],
        scratch_shapes=[pltpu.VMEM((IP_TM, D_MODEL), BF16)],
        compiler_params=_cparams(("parallel", "arbitrary")),
        name="inproj",
    )(xp, xs, mod4, mod4, g, w_big, w_small, b_small, cw)


def _ssd_body(has_h0, emit_state, *refs):
    refs = list(refs)
    xs_ref, bc_ref, sm_ref, tri_ref, alog_ref, exp_ref = refs[:6]
    k = 6
    h0_ref = None
    if has_h0:
        h0_ref = refs[k]
        k += 1
    y_ref = refs[k]
    k += 1
    hfin_ref = None
    if emit_state:
        hfin_ref = refs[k]
        k += 1
    hs_scr = refs[k]

    d = pl.program_id(0)
    c = pl.program_id(2)
    nc = pl.num_programs(2)

    @pl.when(c == 0)
    def _():
        if has_h0:
            hs_scr[...] = h0_ref[...].T
        else:
            hs_scr[...] = jnp.zeros(hs_scr.shape, F32)

    tri = tri_ref[...]
    keep = tri > 0.0
    sm = sm_ref[...]
    lane = lax.broadcasted_iota(jnp.int32, sm.shape, 1)
    dt = jnp.where(d == 0, sm, pltpu.roll(sm, SMALL_COLS - SSD_HEADS, 1))
    a_row = -jnp.exp(alog_ref[...])
    dta = jnp.where(lane < SSD_HEADS, dt * a_row, 0.0)
    cum = jnp.dot(tri, dta, precision=HIGHEST, preferred_element_type=F32)
    cum_t = cum.T
    tot = jnp.where(d == 0, cum[CHUNK - 1:CHUNK, :], cum[0:1, :])
    per_head = jnp.concatenate([jnp.where(lane < SSD_HEADS, dt, 0.0), jnp.exp(cum), jnp.exp(tot - cum),
                                jnp.broadcast_to(jnp.exp(tot), (8, SMALL_COLS))], axis=0)
    hi = per_head.astype(BF16)
    lo = (per_head - hi.astype(F32)).astype(BF16)
    expand = exp_ref[...]
    wide = (jnp.dot(hi, expand, preferred_element_type=F32)
            + jnp.dot(lo, expand, preferred_element_type=F32))
    dt_w = wide[0:CHUNK]
    decay_in_w = wide[CHUNK:2 * CHUNK]
    decay_out_w = wide[2 * CHUNK:3 * CHUNK]
    decay_tot_w = wide[3 * CHUNK:3 * CHUNK + 1]
    xdt = xs_ref[...].astype(F32) * dt_w
    xdt_b = xdt.astype(BF16)
    xdec_b = (xdt * decay_out_w).astype(BF16)
    low_half = lax.broadcasted_iota(jnp.int32, (CHUNK, 2 * SSD_HEAD_DIM), 1) < SSD_HEAD_DIM
    zero_b = jnp.zeros((CHUNK, 2 * SSD_HEAD_DIM), BF16)

    bc = bc_ref[...]
    hpg = SSD_HEADS // SSD_GROUPS
    gw = hpg * SSD_HEAD_DIM
    for g in range(SSD_GROUPS):
        b_g = bc[:, g * SSD_STATE:(g + 1) * SSD_STATE]
        c_g = bc[:, (SSD_GROUPS + g) * SSD_STATE:(SSD_GROUPS + g + 1) * SSD_STATE]
        cb = lax.dot_general(c_g, b_g, (((1,), (1,)), ((), ())), preferred_element_type=F32)
        gs = slice(g * gw, (g + 1) * gw)
        hs_g = hs_scr[:, gs]
        y_off = jnp.dot(c_g, hs_g.astype(BF16), preferred_element_type=F32) * decay_in_w[:, gs]
        for pr in range(hpg // 2):
            h0 = g * hpg + 2 * pr
            ps = slice(h0 * SSD_HEAD_DIM, (h0 + 2) * SSD_HEAD_DIM)
            s_pair = []
            for h in (h0, h0 + 1):
                lmat = jnp.exp(jnp.where(keep, cum[:, h:h + 1] - cum_t[h:h + 1, :], NEG))
                s_pair.append((cb * lmat).astype(BF16))
            x_pair = xdt_b[:, ps]
            rhs = jnp.concatenate([jnp.where(low_half, x_pair, zero_b),
                                   jnp.where(low_half, zero_b, x_pair)], axis=0)
            y_ref[:, ps] = (jnp.dot(jnp.concatenate(s_pair, axis=1), rhs, preferred_element_type=F32)
                            + y_off[:, 2 * pr * SSD_HEAD_DIM:(2 * pr + 2) * SSD_HEAD_DIM])
        upd = lax.dot_general(b_g, xdec_b[:, gs], (((0,), (0,)), ((), ())), preferred_element_type=F32)
        hs_scr[:, gs] = decay_tot_w[:, gs] * hs_g + upd

    if emit_state:
        @pl.when(c == nc - 1)
        def _():
            hfin_ref[...] = hs_scr[...].T


def _chunk_index(cbase, nc):
    def f(d, b, c):
        return cbase + b * nc + c + d * (nc - 1 - 2 * c)
    return f


def _ssd(p_big, small, tri, a_log, h0, cbase, nseq, nc, emit_state):
    ci = _chunk_index(cbase, nc)
    co = _chunk_index(0, nc)
    has_h0 = h0 is not None
    in_specs = [pl.BlockSpec((CHUNK, D_MODEL), lambda d, b, c: (ci(d, b, c), 1)),
                pl.BlockSpec((CHUNK, D_MODEL), lambda d, b, c: (ci(d, b, c), 2)),
                pl.BlockSpec((CHUNK, SMALL_COLS), lambda d, b, c: (ci(d, b, c), 0)),
                pl.BlockSpec((None, CHUNK, CHUNK), lambda d, b, c: (d, 0, 0)),
                pl.BlockSpec((None, 1, SMALL_COLS), lambda d, b, c: (d, 0, 0)),
                pl.BlockSpec((SMALL_COLS, SSD_WIDTH), lambda d, b, c: (0, 0))]
    head_of_lane = np.arange(SSD_WIDTH) // SSD_HEAD_DIM
    expand = jnp.asarray(np.arange(SMALL_COLS)[:, None] == head_of_lane[None, :], BF16)
    args = [p_big, p_big, small, tri, a_log, expand]
    if has_h0:
        in_specs.append(pl.BlockSpec((None, None, SSD_WIDTH, SSD_STATE), lambda d, b, c: (b, d, 0, 0)))
        args.append(h0)
    out_specs = [pl.BlockSpec((None, CHUNK, SSD_WIDTH), lambda d, b, c: (d, co(d, b, c), 0))]
    out_shape = [jax.ShapeDtypeStruct((2, nseq * nc * CHUNK, SSD_WIDTH), F32)]
    if emit_state:
        out_specs.append(pl.BlockSpec((None, None, SSD_WIDTH, SSD_STATE), lambda d, b, c: (b, d, 0, 0)))
        out_shape.append(jax.ShapeDtypeStruct((nseq, 2, SSD_WIDTH, SSD_STATE), F32))
    return pl.pallas_call(
        functools.partial(_ssd_body, has_h0, emit_state),
        grid=(2, nseq, nc),
        in_specs=in_specs,
        out_specs=out_specs,
        out_shape=out_shape,
        scratch_shapes=[pltpu.VMEM((SSD_STATE, SSD_WIDTH), F32)],
        compiler_params=_cparams(("arbitrary", "arbitrary", "arbitrary")),
        name="ssd_state" if emit_state else "ssd_seeded",
    )(*args)


def _mlstm_body(has_init, emit_state, *refs):
    refs = list(refs)
    q_ref, k_ref, v_ref, gc_ref, gr_ref, tri_ref, trit_ref = refs[:7]
    k_i = 7
    c0_ref = n0_ref = m0_ref = None
    if has_init:
        c0_ref, n0_ref, m0_ref = refs[k_i:k_i + 3]
        k_i += 3
    hm_ref = refs[k_i]
    k_i += 1
    cfin_ref = nfin_ref = mfin_ref = None
    if emit_state:
        cfin_ref, nfin_ref, mfin_ref = refs[k_i:k_i + 3]
        k_i += 3
    c_scr, n_scr, m_scr = refs[k_i:k_i + 3]

    c = pl.program_id(2)
    nc = pl.num_programs(2)

    @pl.when(c == 0)
    def _():
        if has_init:
            c_scr[...] = c0_ref[...]
            n_scr[...] = n0_ref[...]
            m_scr[...] = m0_ref[...]
        else:
            c_scr[...] = jnp.zeros(c_scr.shape, F32)
            n_scr[...] = jnp.zeros(n_scr.shape, F32)
            m_scr[...] = jnp.zeros(m_scr.shape, F32)

    tri = tri_ref[...]
    tri_t = trit_ref[...]
    keep = tri > 0.0
    gc = gc_ref[...]
    gr = gr_ref[...]
    dh = MLSTM_HEAD_DIM
    nh = MLSTM_HEADS
    heads = range(nh)
    sls = [slice(hh * dh, (hh + 1) * dh) for hh in heads]
    i_col = [gc[:, hh:hh + 1] for hh in heads]
    f_col = [gc[:, nh + hh:nh + hh + 1] for hh in heads]
    i_row = [gr[hh:hh + 1, :] for hh in heads]
    f_row = [gr[nh + hh:nh + hh + 1, :] for hh in heads]
    b_col = [jnp.sum(tri * f_row[hh], axis=1, keepdims=True) for hh in heads]
    b_row = [jnp.sum(tri_t * f_col[hh], axis=0, keepdims=True) for hh in heads]
    f_tot = [jnp.sum(f_row[hh], axis=1, keepdims=True) for hh in heads]
    m_prev = [m_scr[hh][:, 0:1] for hh in heads]
    q_h = [q_ref[:, sls[hh]] for hh in heads]
    k_h = [k_ref[:, sls[hh]] * (dh ** -0.5) for hh in heads]
    v_h = [v_ref[:, sls[hh]] for hh in heads]
    qk = [lax.dot_general(q_h[hh], k_h[hh], (((1,), (1,)), ((), ())), preferred_element_type=F32) for hh in heads]
    dlog = [jnp.where(keep, b_col[hh] - b_row[hh] + i_row[hh], NEG) for hh in heads]
    inter = [b_col[hh] + m_prev[hh] for hh in heads]
    m_t = [jnp.maximum(inter[hh], jnp.max(dlog[hh], axis=1, keepdims=True)) for hh in heads]
    s = [qk[hh] * jnp.exp(dlog[hh] - m_t[hh]) for hh in heads]
    w_int = [jnp.exp(inter[hh] - m_t[hh]) for hh in heads]
    c_prev = [c_scr[hh] for hh in heads]
    n_prev = [n_scr[hh] for hh in heads]
    num, den = [], []
    for hh in heads:
        q_f = q_h[hh].astype(F32)
        lhs = jnp.concatenate([s[hh].astype(BF16), (q_f * w_int[hh]).astype(BF16)], axis=1)
        rhs = jnp.concatenate([v_h[hh], c_prev[hh].astype(BF16)], axis=0)
        num.append(jnp.dot(lhs, rhs, preferred_element_type=F32))
        den.append(jnp.sum(s[hh], axis=1, keepdims=True)
                   + w_int[hh] * jnp.sum(q_f * n_prev[hh], axis=1, keepdims=True))
    g_col = [f_tot[hh] - b_col[hh] + i_col[hh] for hh in heads]
    m_loc = [jnp.max(g_col[hh], axis=0, keepdims=True) for hh in heads]
    kw = [k_h[hh].astype(F32) * jnp.exp(g_col[hh] - m_loc[hh]) for hh in heads]
    c_loc = [lax.dot_general(kw[hh].astype(BF16), v_h[hh], (((0,), (0,)), ((), ())), preferred_element_type=F32)
             for hh in heads]
    for hh in heads:
        hm_ref[:, sls[hh]] = num[hh] / jnp.maximum(jnp.abs(den[hh]), jnp.exp(-m_t[hh]))
    for hh in heads:
        n_loc = jnp.sum(kw[hh], axis=0, keepdims=True)
        m_new = jnp.maximum(f_tot[hh] + m_prev[hh], m_loc[hh])
        wa = jnp.exp(f_tot[hh] + m_prev[hh] - m_new)
        wb = jnp.exp(m_loc[hh] - m_new)
        c_scr[hh] = wa * c_prev[hh] + wb * c_loc[hh]
        n_scr[hh] = wa * n_prev[hh] + wb * n_loc
        m_scr[hh] = jnp.broadcast_to(m_new, (1, 128))

    if emit_state:
        @pl.when(c == nc - 1)
        def _():
            cfin_ref[...] = c_scr[...]
            nfin_ref[...] = n_scr[...]
            mfin_ref[...] = m_scr[...]


def _mlstm(p_big, gcol, grow, tri, init, cbase, nseq, nc, emit_state):
    ci = _chunk_index(cbase, nc)
    co = _chunk_index(0, nc)
    has_init = init is not None
    dh, nh = MLSTM_HEAD_DIM, MLSTM_HEADS
    in_specs = [pl.BlockSpec((CHUNK, D_MODEL), lambda d, b, c: (ci(d, b, c), 3)),
                pl.BlockSpec((CHUNK, D_MODEL), lambda d, b, c: (ci(d, b, c), 4)),
                pl.BlockSpec((CHUNK, D_MODEL), lambda d, b, c: (ci(d, b, c), 5)),
                pl.BlockSpec((None, CHUNK, 2 * nh), lambda d, b, c: (d, ci(d, b, c), 0)),
                pl.BlockSpec((None, None, 2 * nh, CHUNK), lambda d, b, c: (d, ci(d, b, c), 0, 0)),
                pl.BlockSpec((None, CHUNK, CHUNK), lambda d, b, c: (d, 0, 0)),
                pl.BlockSpec((None, CHUNK, CHUNK), lambda d, b, c: (1 - d, 0, 0))]
    args = [p_big, p_big, p_big, gcol, grow, tri, tri]
    if has_init:
        in_specs += [pl.BlockSpec((None, None, nh, dh, dh), lambda d, b, c: (b, d, 0, 0, 0)),
                     pl.BlockSpec((None, None, nh, 1, dh), lambda d, b, c: (b, d, 0, 0, 0)),
                     pl.BlockSpec((None, None, nh, 1, 128), lambda d, b, c: (b, d, 0, 0, 0))]
        args += list(init)
    out_specs = [pl.BlockSpec((None, CHUNK, D_MODEL), lambda d, b, c: (d, co(d, b, c), 0))]
    out_shape = [jax.ShapeDtypeStruct((2, nseq * nc * CHUNK, D_MODEL), F32)]
    if emit_state:
        out_specs += [pl.BlockSpec((None, None, nh, dh, dh), lambda d, b, c: (b, d, 0, 0, 0)),
                      pl.BlockSpec((None, None, nh, 1, dh), lambda d, b, c: (b, d, 0, 0, 0)),
                      pl.BlockSpec((None, None, nh, 1, 128), lambda d, b, c: (b, d, 0, 0, 0))]
        out_shape += [jax.ShapeDtypeStruct((nseq, 2, nh, dh, dh), F32),
                      jax.ShapeDtypeStruct((nseq, 2, nh, 1, dh), F32),
                      jax.ShapeDtypeStruct((nseq, 2, nh, 1, 128), F32)]
    return pl.pallas_call(
        functools.partial(_mlstm_body, has_init, emit_state),
        grid=(2, nseq, nc),
        in_specs=in_specs,
        out_specs=out_specs,
        out_shape=out_shape,
        scratch_shapes=[pltpu.VMEM((nh, dh, dh), F32), pltpu.VMEM((nh, 1, dh), F32), pltpu.VMEM((nh, 1, 128), F32)],
        compiler_params=_cparams(("arbitrary", "arbitrary", "arbitrary")),
        name="mlstm_state" if emit_state else "mlstm_seeded",
    )(*args)


MG_TM = 256


def _merge_body(n_prompt_tiles, yp_ref, ys_ref, hmp_ref, hms_ref, xp_ref, xsm_ref,
                z_ref, xs_ref, o_ref, g1_ref, g2_ref, gta_ref, shf_ref, scf_ref,
                wso_ref, wmo_ref, wout_ref, gssd_ref, gml_ref, gpost_ref, gpre_ref, dskip_ref,
                x1_ref, h2b_ref):
    is_prompt = pl.program_id(0) < n_prompt_tiles
    y = jnp.where(is_prompt, yp_ref[0] + yp_ref[1], ys_ref[0] + ys_ref[1])
    y = y + dskip_ref[...] * xs_ref[...].astype(F32)
    y = _rms(y * z_ref[...].astype(F32), gssd_ref[...])
    y_ssd = jnp.dot(y.astype(BF16), wso_ref[...], preferred_element_type=F32)
    hm = jnp.where(is_prompt, hmp_ref[0] + hmp_ref[1], hms_ref[0] + hms_ref[1])
    gml = gml_ref[...]
    parts = []
    for hh in range(MLSTM_HEADS):
        sl = slice(hh * MLSTM_HEAD_DIM, (hh + 1) * MLSTM_HEAD_DIM)
        parts.append(_rms(hm[:, sl], gml[:, sl]))
    hm = jnp.concatenate(parts, axis=1) * o_ref[...].astype(F32)
    y_ml = jnp.dot(hm.astype(BF16), wmo_ref[...], preferred_element_type=F32)
    mixed = g1_ref[...].astype(F32) * y_ssd + g2_ref[...].astype(F32) * y_ml
    mixed = jnp.dot(mixed.astype(BF16), wout_ref[...], preferred_element_type=F32)
    x = jnp.where(is_prompt, xp_ref[...], xsm_ref[...])
    x1 = x + gta_ref[...] * _rms(mixed, gpost_ref[...])
    x1_ref[...] = x1
    h2 = _rms(x1, gpre_ref[...]) * (1.0 + scf_ref[...]) + shf_ref[...]
    h2b_ref[...] = h2.astype(BF16)


def _merge(y_p, y_s, hm_p, hm_s, xp, xs, p_big, mod4, wso, wmo, wout, gssd, gml, gpost, gpre, dskip,
           row_of_tile):
    t = xp.shape[0] + xs.shape[0]
    tm = MG_TM
    npt = xp.shape[0] // tm
    tok = lambda col: pl.BlockSpec((tm, D_MODEL), lambda i: (i, col))
    both_p = pl.BlockSpec((2, tm, D_MODEL), lambda i: (0, jnp.minimum(i, npt - 1), 0))
    both_s = pl.BlockSpec((2, tm, D_MODEL), lambda i: (0, jnp.maximum(i - npt, 0), 0))
    modrow = lambda idx: pl.BlockSpec((None, None, 1, D_MODEL), lambda i: (row_of_tile(i), idx, 0, 0))
    vec = pl.BlockSpec((1, D_MODEL), lambda i: (0, 0))
    wspec = pl.BlockSpec((D_MODEL, D_MODEL), lambda i: (0, 0))
    return pl.pallas_call(
        functools.partial(_merge_body, npt),
        grid=(t // tm,),
        in_specs=[both_p, both_s, both_p, both_s] + _split_specs(tm, npt) + [
                  tok(0), tok(1), tok(6), tok(7), tok(8),
                  modrow(2), modrow(3), modrow(4),
                  wspec, wspec, wspec, vec, vec, vec, vec, vec],
        out_specs=[tok(0), tok(0)],
        out_shape=[jax.ShapeDtypeStruct((t, D_MODEL), F32),
                   jax.ShapeDtypeStruct((t, D_MODEL), BF16)],
        compiler_params=_cparams(("parallel",)),
        name="merge",
    )(y_p, y_s, hm_p, hm_s, xp, xs, p_big, p_big, p_big, p_big, p_big, mod4, mod4, mod4,
      wso, wmo, wout, gssd, gml, gpost, gpre, dskip)


RT_T = 256


def _top16(s, nrows):
    t = s.shape[1]
    rowid = lax.broadcasted_iota(jnp.int32, (nrows, t), 0).astype(F32)
    row16 = lax.broadcasted_iota(jnp.int32, (PEER_TOPK, t), 0)

    def body(j, carry):
        cur, rank, vals = carry
        m = jnp.max(cur, axis=0, keepdims=True)
        idx = jnp.min(jnp.where(cur == m, rowid, 1e9), axis=0, keepdims=True)
        sel = rowid == idx
        rank = jnp.where(sel, lax.convert_element_type(j, F32), rank)
        cur = jnp.where(sel, -jnp.inf, cur)
        vals = jnp.where(row16 == j, m, vals)
        return cur, rank, vals

    init = (s, jnp.full((nrows, t), 127.0, F32), jnp.zeros((PEER_TOPK, t), F32))
    _, rank, vals = lax.fori_loop(0, PEER_TOPK, body, init)
    return rank, vals


MARK = 2.0 ** 100


def _top16_pair_marked(sa, sb):
    lanes = sa.shape[1]
    row16 = lax.broadcasted_iota(jnp.int32, (PEER_TOPK, lanes), 0)

    def body(j, carry):
        ca, cb, va, vb = carry
        mk = -(lax.convert_element_type(j, F32) + 1.0) * MARK
        ma = jnp.max(ca, axis=0, keepdims=True)
        mb = jnp.max(cb, axis=0, keepdims=True)
        ca = jnp.where(ca == ma, mk, ca)
        cb = jnp.where(cb == mb, mk, cb)
        va = jnp.where(row16 == j, ma, va)
        vb = jnp.where(row16 == j, mb, vb)
        return ca, cb, va, vb

    zeros = jnp.zeros((PEER_TOPK, lanes), F32)
    return lax.fori_loop(0, PEER_TOPK, body, (sa, sb, zeros, zeros))


def _marked_rank(s, cur):
    took = cur <= -MARK
    rank = jnp.where(took, cur * (-1.0 / MARK) - 1.0, 127.0)
    n_took = jnp.sum(took.astype(F32), axis=0, keepdims=True)
    bad = (n_took != float(PEER_TOPK)) | (jnp.min(s, axis=0, keepdims=True) <= -0.25 * MARK)
    return rank, bad.astype(F32)


def _chosen_marked(cand):
    def body(j, cur):
        m = jnp.max(cur, axis=0, keepdims=True)
        return jnp.where(cur == m, -2.0 * MARK, cur)

    cur = lax.fori_loop(0, PEER_TOPK, body, cand)
    took = cur <= -MARK
    n_took = jnp.sum(took.astype(F32), axis=0, keepdims=True)
    return took, (n_took != float(PEER_TOPK)).astype(F32)


def _route_tail(marked, h, s1, s2, rank1, rank2, v1, v2,
                sel1_ref, sel2_ref, sel1t_ref, r2_ref, e2_ref, c1_ref, e1_ref):
    cand = (jnp.dot(sel1_ref[...], v1, precision=HIGHEST, preferred_element_type=F32)
            + jnp.dot(sel2_ref[...], v2, precision=HIGHEST, preferred_element_type=F32))
    crow = lax.broadcasted_iota(jnp.int32, cand.shape, 0)
    if marked:
        chosen, bad = _chosen_marked(jnp.where(crow < N_CAND, cand, -0.75 * MARK))
    else:
        cand = jnp.where(crow < N_CAND, cand, -jnp.inf)
        crank, _ = _top16(cand, N_CAND_PAD)
        chosen = crank < float(PEER_TOPK)
        bad = jnp.zeros((1, cand.shape[1]), F32)
    cmax = v1[0:1, :] + v2[0:1, :]
    z = jnp.sum(jnp.where(chosen, jnp.exp(cand - cmax), 0.0), axis=0, keepdims=True)
    cnt = jnp.dot(sel1t_ref[...], chosen.astype(F32), preferred_element_type=F32)
    c1 = jnp.zeros(s1.shape, F32)
    for j in range(PEER_TOPK):
        c1 = jnp.where(rank1 == float(j), cnt[j:j + 1, :], c1)
    r2_ref[h] = rank2.astype(BF16)
    e2_ref[h] = jnp.exp(s2 - v2[0:1, :]).astype(BF16)
    c1_ref[h] = c1
    e1_ref[h] = jnp.exp(s1 - v1[0:1, :]) / z
    return bad


def _route_body(h2_ref, m_ref, sel1_ref, sel2_ref, sel1t_ref, r2_ref, e2_ref, c1_ref, e1_ref, sc_scr):
    sc_scr[...] = lax.dot_general(m_ref[...], h2_ref[...], (((1,), (1,)), ((), ())), preferred_element_type=F32)
    nk = PEER_N_KEYS
    tail_refs = (sel1_ref, sel2_ref, sel1t_ref, r2_ref, e2_ref, c1_ref, e1_ref)
    for h in range(PEER_HEADS):
        s1 = sc_scr[h * 2 * nk:h * 2 * nk + nk, :]
        s2 = sc_scr[h * 2 * nk + nk:(h + 1) * 2 * nk, :]
        ranks1, ranks2, vals1, vals2 = [], [], [], []
        n_bad = jnp.zeros((1, 128), F32)
        for sub in range(RT_T // 128):
            ls = slice(sub * 128, (sub + 1) * 128)
            ca, cb, va, vb = _top16_pair_marked(s1[:, ls], s2[:, ls])
            ra, bad_a = _marked_rank(s1[:, ls], ca)
            rb, bad_b = _marked_rank(s2[:, ls], cb)
            n_bad = n_bad + bad_a + bad_b
            ranks1.append(ra)
            ranks2.append(rb)
            vals1.append(va)
            vals2.append(vb)
        cat = lambda parts: jnp.concatenate(parts, axis=1)
        bad_c = _route_tail(True, h, s1, s2, cat(ranks1), cat(ranks2), cat(vals1), cat(vals2), *tail_refs)

        @pl.when(jnp.sum(n_bad) + jnp.sum(bad_c) > 0.0)
        def _():
            rank1, v1 = _top16(s1, nk)
            rank2, v2 = _top16(s2, nk)
            _route_tail(False, h, s1, s2, rank1, rank2, v1, v2, *tail_refs)


def _fold_body(k_ref, wqt_ref, m_ref):
    m_ref[...] = jnp.dot(k_ref[...], wqt_ref[...], precision=HIGHEST, preferred_element_type=F32).astype(BF16)


def _fold_keys(keys, wqt):
    nk = PEER_N_KEYS
    nblk = wqt.shape[0] // nk
    return pl.pallas_call(
        _fold_body,
        grid=(nblk,),
        in_specs=[pl.BlockSpec((None, nk, nk), lambda r: (r % 2, 0, 0)),
                  pl.BlockSpec((nk, D_MODEL), lambda r: (r, 0))],
        out_specs=pl.BlockSpec((nk, D_MODEL), lambda r: (r, 0)),
        out_shape=jax.ShapeDtypeStruct(wqt.shape, BF16),
        compiler_params=_cparams(("parallel",)),
        name="fold_keys",
    )(keys, wqt)


def _route(h2b, m_fold, sel1, sel2, sel1t):
    t = h2b.shape[0]
    nk = PEER_N_KEYS
    full = lambda a: pl.BlockSpec(a.shape, lambda i: (0,) * a.ndim)
    ospec = pl.BlockSpec((PEER_HEADS, nk, RT_T), lambda i: (0, 0, i))
    return pl.pallas_call(
        _route_body,
        grid=(t // RT_T,),
        in_specs=[pl.BlockSpec((RT_T, D_MODEL), lambda i: (i, 0)),
                  full(m_fold), full(sel1), full(sel2), full(sel1t)],
        out_specs=[ospec, ospec, ospec, ospec],
        out_shape=[jax.ShapeDtypeStruct((PEER_HEADS, nk, t), BF16),
                   jax.ShapeDtypeStruct((PEER_HEADS, nk, t), BF16),
                   jax.ShapeDtypeStruct((PEER_HEADS, nk, t), F32),
                   jax.ShapeDtypeStruct((PEER_HEADS, nk, t), F32)],
        scratch_shapes=[pltpu.VMEM((2 * nk * PEER_HEADS, RT_T), F32)],
        compiler_params=_cparams(("parallel",)),
        name="route",
    )(h2b, m_fold, sel1, sel2, sel1t)


PE_T = 512
PE_E = 1024
PE_MM1 = 2
SQRT_HALF = float(np.sqrt(0.5))


def _peer_body(h2_ref, u_ref, vt_ref, r2_ref, e2_ref, c1_ref, e1_ref, x1_ref, gtf_ref, gpost_ref,
               o_ref, acc_scr, gt_scr, a_scr):
    e = pl.program_id(1)
    ne = pl.num_programs(1)

    @pl.when(e == 0)
    def _():
        acc_scr[...] = jnp.zeros(acc_scr.shape, F32)

    nk = PEER_N_KEYS
    n_ii = PE_E // nk
    for ii in range(n_ii):
        if ii % PE_MM1 == 0:
            rows = slice(ii * nk, (ii + PE_MM1) * nk)
            a_scr[rows, :] = lax.dot_general(u_ref[rows, :], h2_ref[...], (((1,), (1,)), ((), ())),
                                             preferred_element_type=F32)
        i1 = e * n_ii + ii
        a = a_scr[ii * nk:(ii + 1) * nk, :]
        act = 0.5 * a * (1.0 + lax.erf(a * SQRT_HALF))
        p = jnp.zeros((nk // 16, 16, PE_T), BF16)
        for h in range(PEER_HEADS):
            c1t = jnp.broadcast_to(c1_ref[h, pl.ds(i1, 1), :], (16, PE_T)).astype(BF16)
            e1t = jnp.broadcast_to(e1_ref[h, pl.ds(i1, 1), :], (16, PE_T)).astype(BF16)
            r2 = r2_ref[h].reshape(nk // 16, 16, PE_T)
            e2 = e2_ref[h].reshape(nk // 16, 16, PE_T)
            p = p + jnp.where(r2 < c1t[None], e2 * e1t[None], jnp.zeros((), BF16))
        gt_scr[ii * nk:(ii + 1) * nk, :] = act.astype(BF16) * p.reshape(nk, PE_T)
    acc_scr[...] += jnp.dot(vt_ref[...], gt_scr[...], preferred_element_type=F32)

    @pl.when(e == ne - 1)
    def _():
        f = acc_scr[...].T
        o_ref[...] = x1_ref[...] + gtf_ref[...] * _rms(f, gpost_ref[...])


def _peer(h2b, u_b, vt_b, r2, e2, c1, e1, x1, mod4, gpost, row_of_tile):
    t = h2b.shape[0]
    nk = PEER_N_KEYS
    rspec = pl.BlockSpec((PEER_HEADS, nk, PE_T), lambda i, k: (0, 0, i))
    return pl.pallas_call(
        _peer_body,
        grid=(t // PE_T, PEER_EXPERTS // PE_E),
        in_specs=[pl.BlockSpec((PE_T, D_MODEL), lambda i, k: (i, 0)),
                  pl.BlockSpec((PE_E, D_MODEL), lambda i, k: (k, 0)),
                  pl.BlockSpec((D_MODEL, PE_E), lambda i, k: (0, k)),
                  rspec, rspec, rspec, rspec,
                  pl.BlockSpec((PE_T, D_MODEL), lambda i, k: (i, 0)),
                  pl.BlockSpec((None, None, 1, D_MODEL), lambda i, k: (row_of_tile(i), 5, 0, 0)),
                  pl.BlockSpec((1, D_MODEL), lambda i, k: (0, 0))],
        out_specs=pl.BlockSpec((PE_T, D_MODEL), lambda i, k: (i, 0)),
        out_shape=jax.ShapeDtypeStruct((t, D_MODEL), F32),
        scratch_shapes=[pltpu.VMEM((D_MODEL, PE_T), F32), pltpu.VMEM((PE_E, PE_T), BF16),
                        pltpu.VMEM((PE_E, PE_T), F32)],
        compiler_params=_cparams(("parallel", "arbitrary")),
        name="peer",
    )(h2b, u_b, vt_b, r2, e2, c1, e1, x1, mod4, gpost)


def _row_of_tile(tile_tokens, n_prompt_tokens, sample_len):
    n_prompt_tiles = n_prompt_tokens // tile_tokens
    tiles_per_seq = sample_len // tile_tokens

    def f(i):
        return jnp.where(i < n_prompt_tiles, 0, 1 + (i - n_prompt_tiles) // tiles_per_seq)
    return f


def kernel(x_prompt, x_sample, c, state_ssd, state_mlstm_c, state_mlstm_n, state_mlstm_m, c_ctx, w_mod, b_mod, g_mix_pre, g_mix_post, g_ffn_pre, g_ffn_post, w_in, conv_w, conv_b, ssd_a_log, ssd_dt_bias, ssd_d, ssd_norm_g, w_ssd_out, mlstm_i_bias, mlstm_f_bias, mlstm_norm_g, w_mlstm_out, w_out, peer_w_q, peer_keys1, peer_keys2, peer_u, peer_v):
    assert w_mod.shape[0] == 1, "single trunk layer"
    bp, lp_, d = x_prompt.shape
    bs, ls, _ = x_sample.shape
    assert d == D_MODEL and lp_ == SEQ and ls % IP_TM == 0 and (bp * lp_) % IP_TM == 0
    n_p, n_s = bp * lp_, bs * ls
    xp = x_prompt.reshape(n_p, d)
    xs = x_sample.reshape(n_s, d)

    cvec = jnp.zeros((8, d), F32).at[0].set(c_ctx).at[1:1 + bs].set(c)
    mod = _adaln(cvec, w_mod[0], b_mod[0])
    mod4 = mod.reshape(8, N_MOD, 1, d)

    wi = w_in[0]
    o_z, o_xbc, o_dt = 0, SSD_WIDTH, SSD_WIDTH + 2 * D_MODEL
    o_q = o_dt + 2 * SSD_HEADS
    o_gate = o_q + 4 * D_MODEL
    o_merge = o_gate + 4 * MLSTM_HEADS
    w_big = jnp.concatenate([wi[:, o_z:o_dt], wi[:, o_q:o_gate], wi[:, o_merge:]], axis=1).astype(BF16)
    n_small = 2 * SSD_HEADS + 4 * MLSTM_HEADS
    w_small = jnp.concatenate([wi[:, o_dt:o_q], wi[:, o_gate:o_merge],
                               jnp.zeros((d, SMALL_COLS - n_small), F32)], axis=1)
    gate_bias = jnp.stack([mlstm_i_bias[0], mlstm_f_bias[0]], axis=1).reshape(-1)
    b_small = jnp.concatenate([ssd_dt_bias[0].reshape(-1), gate_bias,
                               jnp.zeros((SMALL_COLS - n_small,), F32)]).reshape(1, SMALL_COLS)
    cw = jnp.zeros((8, BIG_COLS), F32)
    cw = cw.at[0:3, o_xbc:o_dt].set(conv_w[0]).at[3, o_xbc:o_dt].set(conv_b[0])

    row_ip = _row_of_tile(IP_TM, n_p, ls)
    p_big, small = _inproj(xp, xs, mod4, g_mix_pre, w_big, w_small, b_small, cw, row_ip, n_p // IP_TM)

    lo = np.tril(np.ones((CHUNK, CHUNK), np.float32))
    tri = jnp.asarray(np.stack([lo, lo.T]))
    a_log = jnp.zeros((2, 1, SMALL_COLS), F32).at[:, 0, :SSD_HEADS].set(ssd_a_log[0])

    ncp, ncs = lp_ // CHUNK, ls // CHUNK
    y_p, st_ssd = _ssd(p_big, small, tri, a_log, None, 0, bp, ncp, True)
    (y_s,) = _ssd(p_big, small, tri, a_log, state_ssd[:, 0].reshape(bs, 2, SSD_WIDTH, SSD_STATE),
                  n_p // CHUNK, bs, ncs, False)

    gates = small[:, 2 * SSD_HEADS:n_small].reshape(n_p + n_s, 2, 2 * MLSTM_HEADS)
    gcol = gates.transpose(1, 0, 2)
    grow = gcol.reshape(2, (n_p + n_s) // CHUNK, CHUNK, 2 * MLSTM_HEADS).transpose(0, 1, 3, 2)
    hm_p, st_c, st_n, st_m = _mlstm(p_big, gcol, grow, tri, None, 0, bp, ncp, True)
    init = (state_mlstm_c[:, 0],
            state_mlstm_n[:, 0].reshape(bs, 2, MLSTM_HEADS, 1, MLSTM_HEAD_DIM),
            jnp.broadcast_to(state_mlstm_m[:, 0][..., None, None], (bs, 2, MLSTM_HEADS, 1, 128)))
    (hm_s,) = _mlstm(p_big, gcol, grow, tri, init, n_p // CHUNK, bs, ncs, False)

    dskip = jnp.repeat(ssd_d[0, 0] + ssd_d[0, 1], SSD_HEAD_DIM).reshape(1, SSD_WIDTH)
    row_mg = _row_of_tile(MG_TM, n_p, ls)
    x1, h2b = _merge(y_p, y_s, hm_p, hm_s, xp, xs, p_big, mod4,
                     w_ssd_out[0].astype(BF16), w_mlstm_out[0].astype(BF16), w_out[0].astype(BF16),
                     ssd_norm_g, mlstm_norm_g, g_mix_post, g_ffn_pre, dskip, row_mg)

    sel1 = np.zeros((N_CAND_PAD, PEER_TOPK), np.float32)
    sel2 = np.zeros((N_CAND_PAD, PEER_TOPK), np.float32)
    for r, (j, l) in enumerate(_CAND):
        sel1[r, j] = 1.0
        sel2[r, l] = 1.0
    m_fold = _fold_keys(jnp.stack([peer_keys1[0], peer_keys2[0]]), peer_w_q[0].T)
    r2, e2, c1, e1 = _route(h2b, m_fold, jnp.asarray(sel1), jnp.asarray(sel2), jnp.asarray(sel1.T.copy()))

    row_pe = _row_of_tile(PE_T, n_p, ls)
    x2 = _peer(h2b, peer_u[0].astype(BF16), peer_v[0].T.astype(BF16), r2, e2, c1, e1, x1, mod4,
               g_ffn_post, row_pe)

    y_prompt = x2[:n_p].reshape(bp, lp_, d)
    y_sample = x2[n_p:].reshape(bs, ls, d)
    new_ssd = st_ssd.reshape(bp, 1, 2, SSD_HEADS, SSD_HEAD_DIM, SSD_STATE)
    new_c = st_c.reshape(bp, 1, 2, MLSTM_HEADS, MLSTM_HEAD_DIM, MLSTM_HEAD_DIM)
    new_n = st_n.reshape(bp, 1, 2, MLSTM_HEADS, MLSTM_HEAD_DIM)
    new_m = st_m[..., 0, 0].reshape(bp, 1, 2, MLSTM_HEADS)
    return (y_prompt, y_sample, new_ssd, new_c, new_n, new_m)
```
